```python
import jax, jax.numpy as jnp
from jax import lax
import numpy as np

D_MODEL = 4096
BATCH = 4
SEQ = 2048
DEPTH = 1

GRID_W = 64
ROPE_THETA = 10000.0
Q_BLOCK = 128
EPS = 1e-6

MLA_HEADS = 16
MLA_NOPE = 128
MLA_ROPE = 64
MLA_V = 128
Q_LORA = 1024
KV_LORA = 512
MLA_SCALE = (MLA_NOPE + MLA_ROPE) ** -0.5

GQA_Q_HEADS = 16
GQA_KV_HEADS = 4
GQA_HEAD_DIM = 128
GQA_SCALE = GQA_HEAD_DIM ** -0.5

N_BRANCH = 2
IN_SIZES = (Q_LORA, KV_LORA, MLA_ROPE, GQA_Q_HEADS * GQA_HEAD_DIM,
            GQA_KV_HEADS * GQA_HEAD_DIM, GQA_KV_HEADS * GQA_HEAD_DIM, N_BRANCH * D_MODEL)
IN_WIDTH = Q_LORA + KV_LORA + MLA_ROPE + (GQA_Q_HEADS + 2 * GQA_KV_HEADS) * GQA_HEAD_DIM + N_BRANCH * D_MODEL

D_FF = ((8 * D_MODEL + 3 * 256 - 1) // (3 * 256)) * 256

kernel_name = "hybrid_gated_mla_gqa_axial_encoder"


def rmsnorm(x, g):
    xf = x.astype(jnp.float32)
    y = xf * lax.rsqrt(jnp.mean(xf * xf, axis=-1, keepdims=True) + EPS)
    return (y * g.astype(jnp.float32)).astype(x.dtype)


def rope_table(pos, dim):
    inv = ROPE_THETA ** (-jnp.arange(0, dim, 2, dtype=jnp.float32) / dim)
    ang = pos.astype(jnp.float32)[:, None] * inv[None, :]
    ang = jnp.concatenate([ang, ang], axis=-1)
    return jnp.cos(ang), jnp.sin(ang)


def apply_rope(x, pos):
    d = x.shape[-1]
    cos, sin = rope_table(pos, d)
    x1, x2 = x[..., : d // 2], x[..., d // 2:]
    rot = jnp.concatenate([-x2, x1], axis=-1)
    y = x.astype(jnp.float32) * cos[None, :, None, :] + rot.astype(jnp.float32) * sin[None, :, None, :]
    return y.astype(x.dtype)


def axial_rope(x, row_idx, col_idx):
    half = x.shape[-1] // 2
    return jnp.concatenate([apply_rope(x[..., :half], row_idx),
                            apply_rope(x[..., half:], col_idx)], axis=-1)


def block_attention(q, k, v, scale):
    B, S, H, Dk = q.shape
    Hk = k.shape[2]
    G = H // Hk
    Dv = v.shape[-1]
    nb = S // Q_BLOCK
    qb = q.reshape(B, nb, Q_BLOCK, Hk, G, Dk).transpose(1, 0, 2, 3, 4, 5)

    def one_block(q_blk):
        s = jnp.einsum("bqhgd,bkhd->bhgqk", q_blk, k,
                       preferred_element_type=jnp.float32) * scale
        p = jax.nn.softmax(s, axis=-1).astype(v.dtype)
        return jnp.einsum("bhgqk,bkhd->bqhgd", p, v)

    o = lax.map(one_block, qb)
    return o.transpose(1, 0, 2, 3, 4, 5).reshape(B, S, H, Dv)


def setup_inputs(seed: int = 0) -> dict:
    key = jax.random.key(seed)
    ks = iter(jax.random.split(key, 24))

    def dense(fan_in, fan_out):
        return jax.random.normal(next(ks), (DEPTH, fan_in, fan_out), jnp.float32) * fan_in ** -0.5

    def gain(n):
        return 1.0 + 0.01 * jax.random.normal(next(ks), (DEPTH, n), jnp.float32)

    x = jax.random.normal(next(ks), (BATCH, SEQ, D_MODEL), jnp.float32)
    inputs = {}
    inputs["x"] = x
    inputs["g_attn"] = gain(D_MODEL)
    inputs["w_in"] = dense(D_MODEL, IN_WIDTH)
    inputs["g_q_a"] = gain(Q_LORA)
    inputs["w_q_b"] = dense(Q_LORA, MLA_HEADS * (MLA_NOPE + MLA_ROPE))
    inputs["g_kv_a"] = gain(KV_LORA)
    inputs["w_kv_b"] = dense(KV_LORA, MLA_HEADS * (MLA_NOPE + MLA_V))
    inputs["g_qn"] = gain(GQA_HEAD_DIM)
    inputs["g_kn"] = gain(GQA_HEAD_DIM)
    inputs["w_branch_a"] = dense(MLA_HEADS * MLA_V, D_MODEL)
    inputs["w_branch_b"] = dense(GQA_Q_HEADS * GQA_HEAD_DIM, D_MODEL)
    inputs["w_o"] = dense(D_MODEL, D_MODEL)
    inputs["g_ffn"] = gain(D_MODEL)
    inputs["w_gate"] = dense(D_MODEL, D_FF)
    inputs["w_up"] = dense(D_MODEL, D_FF)
    inputs["w_down"] = dense(D_FF, D_MODEL)
    inputs["g_final"] = 1.0 + 0.01 * jax.random.normal(next(ks), (D_MODEL,), jnp.float32)
    return inputs


def reference(x, g_attn, w_in, g_q_a, w_q_b, g_kv_a, w_kv_b, g_qn, g_kn,
              w_branch_a, w_branch_b, w_o, g_ffn, w_gate, w_up, w_down, g_final):
    B, S, _ = x.shape
    rows = S // GRID_W
    row_idx = jnp.repeat(jnp.arange(rows, dtype=jnp.int32), GRID_W)
    col_idx = jnp.tile(jnp.arange(GRID_W, dtype=jnp.int32), rows)
    split_at = [int(v) for v in np.cumsum(IN_SIZES)[:-1]]

    for l in range(DEPTH):
        h = rmsnorm(x, g_attn[l])
        z = h @ w_in[l]
        z_qa, z_kva, z_kpe, z_q, z_k, z_v, z_gate = jnp.split(z, split_at, axis=-1)

        c_q = rmsnorm(z_qa, g_q_a[l])
        q_a = (c_q @ w_q_b[l]).reshape(B, S, MLA_HEADS, MLA_NOPE + MLA_ROPE)
        q_a = jnp.concatenate([q_a[..., :MLA_NOPE],
                               axial_rope(q_a[..., MLA_NOPE:], row_idx, col_idx)], axis=-1)
        c_kv = rmsnorm(z_kva, g_kv_a[l])
        kv = (c_kv @ w_kv_b[l]).reshape(B, S, MLA_HEADS, MLA_NOPE + MLA_V)
        k_nope, v_a = kv[..., :MLA_NOPE], kv[..., MLA_NOPE:]
        k_pe = axial_rope(z_kpe[:, :, None, :], row_idx, col_idx)
        k_a = jnp.concatenate([k_nope, jnp.broadcast_to(k_pe, (B, S, MLA_HEADS, MLA_ROPE))], axis=-1)
        o_a = block_attention(q_a, k_a, v_a, MLA_SCALE).reshape(B, S, MLA_HEADS * MLA_V)

        q_b = rmsnorm(z_q.reshape(B, S, GQA_Q_HEADS, GQA_HEAD_DIM), g_qn[l])
        k_b = rmsnorm(z_k.reshape(B, S, GQA_KV_HEADS, GQA_HEAD_DIM), g_kn[l])
        v_b = z_v.reshape(B, S, GQA_KV_HEADS, GQA_HEAD_DIM)
        q_b = axial_rope(q_b, row_idx, col_idx)
        k_b = axial_rope(k_b, row_idx, col_idx)
        o_b = block_attention(q_b, k_b, v_b, GQA_SCALE).reshape(B, S, GQA_Q_HEADS * GQA_HEAD_DIM)

        gates = jax.nn.sigmoid(z_gate)
        g_a, g_b = gates[..., :D_MODEL], gates[..., D_MODEL:]
        m = g_a * (o_a @ w_branch_a[l]) + g_b * (o_b @ w_branch_b[l])
        x = x + m @ w_o[l]

        h2 = rmsnorm(x, g_ffn[l])
        x = x + (jax.nn.silu(h2 @ w_gate[l]) * (h2 @ w_up[l])) @ w_down[l]

    return rmsnorm(x, g_final)
```

```python
import functools

import jax
import jax.numpy as jnp
import numpy as np
from jax import lax
from jax.experimental import pallas as pl
from jax.experimental.pallas import tpu as pltpu

GRID_W = 64
ROPE_THETA = 10000.0
EPS = 1e-6

MLA_HEADS = 16
MLA_NOPE = 128
MLA_ROPE = 64
MLA_V = 128
MLA_QK_PAD = 256
Q_LORA = 1024
KV_LORA = 512
MLA_SCALE = (MLA_NOPE + MLA_ROPE) ** -0.5

GQA_Q_HEADS = 16
GQA_KV_HEADS = 4
GQA_HEAD_DIM = 128
GQA_SCALE = GQA_HEAD_DIM ** -0.5

LANES = 128
V7X_VMEM_BYTES = 64 * 1024 * 1024
VMEM_CAP_BYTES = V7X_VMEM_BYTES - 6 * 1024 * 1024

BF16 = jnp.bfloat16
F32 = jnp.float32


def _nbytes(shape, dtype):
    return int(np.prod(shape)) * jnp.dtype(dtype).itemsize


def _params(semantics, pipelined_blocks, temporaries=()):
    est = 2 * sum(_nbytes(s, d) for s, d in pipelined_blocks)
    est += sum(_nbytes(s, d) for s, d in temporaries)
    limit = min(VMEM_CAP_BYTES, int(est * 1.25) + (4 << 20))
    return pltpu.CompilerParams(dimension_semantics=semantics, vmem_limit_bytes=limit)


def _rmsnorm_kernel(x_ref, g_ref, o_ref):
    x = x_ref[...]
    ms = jnp.mean(x * x, axis=-1, keepdims=True)
    o_ref[...] = (x * lax.rsqrt(ms + EPS) * g_ref[...]).astype(o_ref.dtype)


def _rmsnorm(x, g, out_dtype, name, block_rows=256):
    t, d = x.shape
    blocks = [((block_rows, d), F32), ((block_rows, d), out_dtype)]
    return pl.pallas_call(
        _rmsnorm_kernel,
        out_shape=jax.ShapeDtypeStruct((t, d), out_dtype),
        grid=(t // block_rows,),
        in_specs=[pl.BlockSpec((block_rows, d), lambda i: (i, 0)),
                  pl.BlockSpec((1, d), lambda i: (0, 0))],
        out_specs=pl.BlockSpec((block_rows, d), lambda i: (i, 0)),
        compiler_params=_params(("parallel",), blocks, [((block_rows, d), F32)] * 2),
        name=name,
    )(x, g.reshape(1, d))


def _dot(a, b):
    return jnp.dot(a, b, preferred_element_type=F32)


def _mm_kernel(a_ref, b_ref, o_ref):
    o_ref[...] = _dot(a_ref[...], b_ref[...]).astype(o_ref.dtype)


def _matmul(a, b, out_dtype, bm, bn, name):
    m, k = a.shape
    n = b.shape[1]
    blocks = [((bm, k), a.dtype), ((k, bn), b.dtype), ((bm, bn), out_dtype)]
    return pl.pallas_call(
        _mm_kernel,
        out_shape=jax.ShapeDtypeStruct((m, n), out_dtype),
        grid=(m // bm, n // bn),
        in_specs=[pl.BlockSpec((bm, k), lambda i, j: (i, 0)),
                  pl.BlockSpec((k, bn), lambda i, j: (0, j))],
        out_specs=pl.BlockSpec((bm, bn), lambda i, j: (i, j)),
        compiler_params=_params(("parallel", "arbitrary"), blocks, [((bm, bn), F32)]),
        name=name,
    )(a, b)


def _mm_residual_kernel(a_ref, b_ref, r_ref, o_ref):
    o_ref[...] = r_ref[...] + _dot(a_ref[...], b_ref[...])


def _matmul_residual(a, b, r, bm, bn, name, a_col_block=0, k_block=None):
    m = a.shape[0]
    k = b.shape[0] if k_block is None else k_block
    n = b.shape[1]
    blocks = [((bm, k), a.dtype), ((k, bn), b.dtype), ((bm, bn), F32), ((bm, bn), F32)]
    return pl.pallas_call(
        _mm_residual_kernel,
        out_shape=jax.ShapeDtypeStruct((m, n), F32),
        grid=(m // bm, n // bn),
        in_specs=[pl.BlockSpec((bm, k), lambda i, j: (i, a_col_block)),
                  pl.BlockSpec((k, bn), lambda i, j: (0, j)),
                  pl.BlockSpec((bm, bn), lambda i, j: (i, j))],
        out_specs=pl.BlockSpec((bm, bn), lambda i, j: (i, j)),
        compiler_params=_params(("parallel", "arbitrary"), blocks, [((bm, bn), F32)]),
        name=name,
    )(a, b, r)


def _sigmoid(x):
    return 1.0 / (1.0 + jnp.exp(-x))


def _merge_kernel(oa_ref, ob_ref, wa_ref, wb_ref, ga_ref, gb_ref, o_ref):
    pa = _dot(oa_ref[...], wa_ref[...])
    pb = _dot(ob_ref[...], wb_ref[...])
    o_ref[...] = (_sigmoid(ga_ref[...]) * pa + _sigmoid(gb_ref[...]) * pb).astype(o_ref.dtype)


def _gated_merge(o_a, o_b, w_a, w_b, z_gate, bm, bn, name):
    t, ka = o_a.shape
    kb = o_b.shape[1]
    n = w_a.shape[1]
    nb = n // bn
    blocks = [((bm, ka), BF16), ((bm, kb), BF16), ((ka, bn), BF16), ((kb, bn), BF16),
              ((bm, bn), F32), ((bm, bn), F32), ((bm, bn), BF16)]
    return pl.pallas_call(
        _merge_kernel,
        out_shape=jax.ShapeDtypeStruct((t, n), BF16),
        grid=(t // bm, nb),
        in_specs=[pl.BlockSpec((bm, ka), lambda i, j: (i, 0)),
                  pl.BlockSpec((bm, kb), lambda i, j: (i, 0)),
                  pl.BlockSpec((ka, bn), lambda i, j: (0, j)),
                  pl.BlockSpec((kb, bn), lambda i, j: (0, j)),
                  pl.BlockSpec((bm, bn), lambda i, j: (i, j)),
                  pl.BlockSpec((bm, bn), lambda i, j: (i, j + nb))],
        out_specs=pl.BlockSpec((bm, bn), lambda i, j: (i, j)),
        compiler_params=_params(("parallel", "arbitrary"), blocks, [((bm, bn), F32)] * 3),
        name=name,
    )(o_a, o_b, w_a, w_b, z_gate, z_gate)


def _swiglu_kernel(a_ref, wg_ref, wu_ref, o_ref):
    a = a_ref[...]
    g = _dot(a, wg_ref[...])
    u = _dot(a, wu_ref[...])
    o_ref[...] = (g * _sigmoid(g) * u).astype(o_ref.dtype)


def _swiglu(a, w_gate, w_up, bm, bn, name):
    t, k = a.shape
    n = w_gate.shape[1]
    blocks = [((bm, k), BF16), ((k, bn), BF16), ((k, bn), BF16), ((bm, bn), BF16)]
    return pl.pallas_call(
        _swiglu_kernel,
        out_shape=jax.ShapeDtypeStruct((t, n), BF16),
        grid=(t // bm, n // bn),
        in_specs=[pl.BlockSpec((bm, k), lambda i, j: (i, 0)),
                  pl.BlockSpec((k, bn), lambda i, j: (0, j)),
                  pl.BlockSpec((k, bn), lambda i, j: (0, j))],
        out_specs=pl.BlockSpec((bm, bn), lambda i, j: (i, j)),
        compiler_params=_params(("parallel", "arbitrary"), blocks, [((bm, bn), F32)] * 3),
        name=name,
    )(a, w_gate, w_up)


def _rope_table(pos, dim):
    inv = ROPE_THETA ** (-jnp.arange(0, dim, 2, dtype=F32) / dim)
    ang = pos.astype(F32)[:, None] * inv[None, :]
    ang = jnp.concatenate([ang, ang], axis=-1)
    return jnp.cos(ang), jnp.sin(ang)


def _axial_tables(seq, rot_dim):
    rows = seq // GRID_W
    row_idx = jnp.repeat(jnp.arange(rows, dtype=jnp.int32), GRID_W)
    col_idx = jnp.tile(jnp.arange(GRID_W, dtype=jnp.int32), rows)
    half = rot_dim // 2
    cos_r, sin_r = _rope_table(row_idx, half)
    cos_c, sin_c = _rope_table(col_idx, half)
    cos = jnp.concatenate([cos_r, cos_c], axis=-1)
    sin = jnp.concatenate([sin_r, sin_c], axis=-1)
    first = (jnp.arange(rot_dim) % half) < (half // 2)
    sin_up = jnp.where(first[None, :], -sin, 0.0)
    sin_dn = jnp.where(first[None, :], 0.0, sin)
    pad = ((0, 0), (0, LANES - rot_dim))
    return tuple(jnp.pad(t, pad) for t in (cos, sin_up, sin_dn))


def _rope_lanes(x, cos, sin_up, sin_dn, quarter):
    up = pltpu.roll(x, LANES - quarter, 1)
    dn = pltpu.roll(x, quarter, 1)
    return x * cos + up * sin_up + dn * sin_dn


def _mla_q_kernel(z_ref, g_ref, w_ref, cos_ref, su_ref, sd_ref, o_ref, cq_ref, *, heads_per_step):
    @pl.when(pl.program_id(1) == 0)
    def _():
        z = z_ref[...]
        ms = jnp.mean(z * z, axis=-1, keepdims=True)
        cq_ref[...] = (z * lax.rsqrt(ms + EPS) * g_ref[...]).astype(cq_ref.dtype)

    q = _dot(cq_ref[...], w_ref[...])
    cos, su, sd = cos_ref[...], su_ref[...], sd_ref[...]
    for h in range(heads_per_step):
        base = h * MLA_QK_PAD
        o_ref[:, base:base + MLA_NOPE] = q[:, base:base + MLA_NOPE].astype(o_ref.dtype)
        pe = q[:, base + MLA_NOPE:base + MLA_QK_PAD]
        o_ref[:, base + MLA_NOPE:base + MLA_QK_PAD] = _rope_lanes(
            pe, cos, su, sd, MLA_ROPE // 4).astype(o_ref.dtype)


def _mla_q_proj(z1, g_q_a, w_q, tables, seq, bm, heads_per_step, name):
    t = z1.shape[0]
    bn = heads_per_step * MLA_QK_PAD
    n = w_q.shape[1]
    s_blocks = seq // bm
    tab_spec = pl.BlockSpec((bm, LANES), lambda i, j: (i % s_blocks, 0))
    blocks = [((bm, Q_LORA), F32), ((Q_LORA, bn), BF16), ((bm, bn), BF16)] + [((bm, LANES), F32)] * 3
    return pl.pallas_call(
        functools.partial(_mla_q_kernel, heads_per_step=heads_per_step),
        out_shape=jax.ShapeDtypeStruct((t, n), BF16),
        grid=(t // bm, n // bn),
        in_specs=[pl.BlockSpec((bm, Q_LORA), lambda i, j: (i, 0)),
                  pl.BlockSpec((1, Q_LORA), lambda i, j: (0, 0)),
                  pl.BlockSpec((Q_LORA, bn), lambda i, j: (0, j)),
                  tab_spec, tab_spec, tab_spec],
        out_specs=pl.BlockSpec((bm, bn), lambda i, j: (i, j)),
        scratch_shapes=[pltpu.VMEM((bm, Q_LORA), BF16)],
        compiler_params=_params(("parallel", "arbitrary"), blocks,
                                [((bm, Q_LORA), BF16), ((bm, bn), F32), ((bm, Q_LORA), F32)]),
        name=name,
    )(z1, g_q_a.reshape(1, Q_LORA), w_q, *tables)


def _mla_kv_kernel(z_ref, pe_ref, g_ref, w_ref, cos_ref, su_ref, sd_ref, k_ref, v_ref, ckv_ref, kpe_ref,
                   *, heads_per_step):
    @pl.when(pl.program_id(1) == 0)
    def _():
        z = z_ref[...]
        ms = jnp.mean(z * z, axis=-1, keepdims=True)
        ckv_ref[...] = (z * lax.rsqrt(ms + EPS) * g_ref[...]).astype(ckv_ref.dtype)
        kpe_ref[...] = _rope_lanes(pe_ref[...], cos_ref[...], su_ref[...], sd_ref[...],
                                   MLA_ROPE // 4).astype(kpe_ref.dtype)

    kv = _dot(ckv_ref[...], w_ref[...])
    for h in range(heads_per_step):
        src = h * (MLA_NOPE + MLA_V)
        dst = h * MLA_QK_PAD
        k_ref[:, dst:dst + MLA_NOPE] = kv[:, src:src + MLA_NOPE].astype(k_ref.dtype)
        k_ref[:, dst + MLA_NOPE:dst + MLA_QK_PAD] = kpe_ref[...]
        v_ref[:, h * MLA_V:(h + 1) * MLA_V] = kv[:, src + MLA_NOPE:src + MLA_NOPE + MLA_V].astype(v_ref.dtype)


def _mla_kv_proj(z1, g_kv_a, w_kv, tables, seq, bm, heads_per_step, name):
    t = z1.shape[0]
    bn = heads_per_step * (MLA_NOPE + MLA_V)
    steps = w_kv.shape[1] // bn
    s_blocks = seq // bm
    tab_spec = pl.BlockSpec((bm, LANES), lambda i, j: (i % s_blocks, 0))
    kva_block = Q_LORA // KV_LORA
    kpe_block = (Q_LORA + KV_LORA) // LANES
    bk, bv = heads_per_step * MLA_QK_PAD, heads_per_step * MLA_V
    blocks = ([((bm, KV_LORA), F32), ((bm, LANES), F32), ((KV_LORA, bn), BF16), ((bm, bk), BF16),
               ((bm, bv), BF16)] + [((bm, LANES), F32)] * 3)
    return pl.pallas_call(
        functools.partial(_mla_kv_kernel, heads_per_step=heads_per_step),
        out_shape=(jax.ShapeDtypeStruct((t, steps * bk), BF16), jax.ShapeDtypeStruct((t, steps * bv), BF16)),
        grid=(t // bm, steps),
        in_specs=[pl.BlockSpec((bm, KV_LORA), lambda i, j: (i, kva_block)),
                  pl.BlockSpec((bm, LANES), lambda i, j: (i, kpe_block)),
                  pl.BlockSpec((1, KV_LORA), lambda i, j: (0, 0)),
                  pl.BlockSpec((KV_LORA, bn), lambda i, j: (0, j)),
                  tab_spec, tab_spec, tab_spec],
        out_specs=(pl.BlockSpec((bm, bk), lambda i, j: (i, j)),
                   pl.BlockSpec((bm, bv), lambda i, j: (i, j))),
        scratch_shapes=[pltpu.VMEM((bm, KV_LORA), BF16), pltpu.VMEM((bm, LANES), BF16)],
        compiler_params=_params(("parallel", "arbitrary"), blocks,
                                [((bm, KV_LORA), BF16), ((bm, bn), F32), ((bm, KV_LORA), F32)]),
        name=name,
    )(z1, z1, g_kv_a.reshape(1, KV_LORA), w_kv, *tables)


def _gqa_prep_kernel(z_ref, g_ref, cos_ref, su_ref, sd_ref, o_ref, *, heads_per_step):
    cos, su, sd = cos_ref[...], su_ref[...], sd_ref[...]
    for h in range(heads_per_step):
        sl = slice(h * GQA_HEAD_DIM, (h + 1) * GQA_HEAD_DIM)
        x = z_ref[:, sl]
        ms = jnp.mean(x * x, axis=-1, keepdims=True)
        y = x * lax.rsqrt(ms + EPS) * g_ref[:, sl]
        o_ref[:, sl] = _rope_lanes(y, cos, su, sd, GQA_HEAD_DIM // 4).astype(o_ref.dtype)


def _gqa_prep(z2, gains, tables, seq, n_heads, br, heads_per_step, name):
    t = z2.shape[0]
    bn = heads_per_step * GQA_HEAD_DIM
    s_blocks = seq // br
    tab_spec = pl.BlockSpec((br, LANES), lambda i, j: (i % s_blocks, 0))
    blocks = [((br, bn), F32), ((br, bn), BF16)] + [((br, LANES), F32)] * 3
    return pl.pallas_call(
        functools.partial(_gqa_prep_kernel, heads_per_step=heads_per_step),
        out_shape=jax.ShapeDtypeStruct((t, n_heads * GQA_HEAD_DIM), BF16),
        grid=(t // br, n_heads // heads_per_step),
        in_specs=[pl.BlockSpec((br, bn), lambda i, j: (i, j)),
                  pl.BlockSpec((1, bn), lambda i, j: (0, j)),
                  tab_spec, tab_spec, tab_spec],
        out_specs=pl.BlockSpec((br, bn), lambda i, j: (i, j)),
        compiler_params=_params(("parallel", "arbitrary"), blocks, [((br, bn), F32)] * 2),
        name=name,
    )(z2, gains, *tables)


def _attention_kernel(q_ref, k_ref, v_ref, o_ref, *, scale):
    s = lax.dot_general(q_ref[...], k_ref[...], (((1,), (1,)), ((), ())),
                        preferred_element_type=F32) * scale
    m = jnp.max(s, axis=-1, keepdims=True)
    p = jnp.exp(s - m)
    l = jnp.sum(p, axis=-1, keepdims=True)
    o = _dot(p.astype(BF16), v_ref[...])
    o_ref[...] = (o / l).astype(o_ref.dtype)


def _attention(q, k, v, *, n_q_heads, group, dk, dv, k_head_offset, scale, bq, name):
    b, s, _ = q.shape
    blocks = [((bq, dk), BF16), ((s, dk), BF16), ((s, dv), BF16), ((bq, dv), BF16)]
    return pl.pallas_call(
        functools.partial(_attention_kernel, scale=scale),
        out_shape=jax.ShapeDtypeStruct((b, s, n_q_heads * dv), BF16),
        grid=(b, n_q_heads, s // bq),
        in_specs=[pl.BlockSpec((None, bq, dk), lambda bi, h, qi: (bi, qi, h)),
                  pl.BlockSpec((None, s, dk), lambda bi, h, qi: (bi, 0, k_head_offset + h // group)),
                  pl.BlockSpec((None, s, dv), lambda bi, h, qi: (bi, 0, h // group))],
        out_specs=pl.BlockSpec((None, bq, dv), lambda bi, h, qi: (bi, qi, h)),
        compiler_params=_params(("parallel", "arbitrary", "arbitrary"), blocks,
                                [((bq, s), F32)] * 3 + [((bq, s), BF16)]),
        name=name,
    )(q, k, v)


def _layer(xf, batch, seq, g_attn, w_in, g_q_a, w_q_b, g_kv_a, w_kv_b, g_qn, g_kn,
           w_branch_a, w_branch_b, w_o, g_ffn, w_gate, w_up, w_down):
    t, d = xf.shape
    seg1 = Q_LORA + KV_LORA + MLA_ROPE
    seg2 = (GQA_Q_HEADS + 2 * GQA_KV_HEADS) * GQA_HEAD_DIM

    w1 = jnp.pad(w_in[:, :seg1], ((0, 0), (0, LANES - MLA_ROPE))).astype(BF16)
    w2 = w_in[:, seg1:seg1 + seg2].astype(BF16)
    w3 = w_in[:, seg1 + seg2:].astype(BF16)
    w_q = jnp.pad(w_q_b.reshape(Q_LORA, MLA_HEADS, MLA_NOPE + MLA_ROPE),
                  ((0, 0), (0, 0), (0, MLA_QK_PAD - MLA_NOPE - MLA_ROPE)))
    w_q = w_q.reshape(Q_LORA, MLA_HEADS * MLA_QK_PAD).astype(BF16)
    w_kv = w_kv_b.astype(BF16)
    gqa_gains = jnp.concatenate([jnp.tile(g_qn, GQA_Q_HEADS), jnp.tile(g_kn, GQA_KV_HEADS)]).reshape(1, -1)
    mla_tables = _axial_tables(seq, MLA_ROPE)
    gqa_tables = _axial_tables(seq, GQA_HEAD_DIM)

    h = _rmsnorm(xf, g_attn, BF16, "norm_attn")
    z1 = _matmul(h, w1, F32, 512, w1.shape[1], "in_proj_latent")
    z2 = _matmul(h, w2, F32, 1024, 1024, "in_proj_gqa")
    z_gate = _matmul(h, w3, F32, 1024, 1024, "in_proj_gate")

    q_a = _mla_q_proj(z1, g_q_a, w_q, mla_tables, seq, 1024, 2, "mla_q_proj")
    k_a, v_a = _mla_kv_proj(z1, g_kv_a, w_kv, mla_tables, seq, 1024, 2, "mla_kv_proj")
    o_a = _attention(q_a.reshape(batch, seq, -1), k_a.reshape(batch, seq, -1), v_a.reshape(batch, seq, -1),
                     n_q_heads=MLA_HEADS, group=1, dk=MLA_QK_PAD, dv=MLA_V, k_head_offset=0,
                     scale=MLA_SCALE, bq=512, name="mla_attention")

    qk_b = _gqa_prep(z2, gqa_gains, gqa_tables, seq, GQA_Q_HEADS + GQA_KV_HEADS, 512, 4, "gqa_prep")
    v_b = z2[:, (GQA_Q_HEADS + GQA_KV_HEADS) * GQA_HEAD_DIM:].astype(BF16)
    qk_b3 = qk_b.reshape(batch, seq, -1)
    o_b = _attention(qk_b3, qk_b3, v_b.reshape(batch, seq, -1),
                     n_q_heads=GQA_Q_HEADS, group=GQA_Q_HEADS // GQA_KV_HEADS, dk=GQA_HEAD_DIM,
                     dv=GQA_HEAD_DIM, k_head_offset=GQA_Q_HEADS, scale=GQA_SCALE, bq=512,
                     name="gqa_attention")

    m = _gated_merge(o_a.reshape(t, -1), o_b.reshape(t, -1), w_branch_a.astype(BF16),
                     w_branch_b.astype(BF16), z_gate, 1024, 512, "gated_merge")
    x1 = _matmul_residual(m, w_o.astype(BF16), xf, 1024, 512, "out_proj")

    h2 = _rmsnorm(x1, g_ffn, BF16, "norm_ffn")
    act = _swiglu(h2, w_gate.astype(BF16), w_up.astype(BF16), 1024, 256, "ffn_gate_up")
    d_ff = act.shape[1]
    k_half = d_ff // 2
    wd = w_down.astype(BF16)
    y = _matmul_residual(act, wd[:k_half], x1, 1024, 512, "ffn_down_lo", a_col_block=0, k_block=k_half)
    return _matmul_residual(act, wd[k_half:], y, 1024, 512, "ffn_down_hi", a_col_block=1, k_block=k_half)


def kernel(x, g_attn, w_in, g_q_a, w_q_b, g_kv_a, w_kv_b, g_qn, g_kn, w_branch_a, w_branch_b, w_o,
           g_ffn, w_gate, w_up, w_down, g_final):
    batch, seq, d = x.shape
    xf = x.reshape(batch * seq, d)
    for l in range(g_attn.shape[0]):
        xf = _layer(xf, batch, seq, g_attn[l], w_in[l], g_q_a[l], w_q_b[l], g_kv_a[l], w_kv_b[l],
                    g_qn[l], g_kn[l], w_branch_a[l], w_branch_b[l], w_o[l], g_ffn[l],
                    w_gate[l], w_up[l], w_down[l])
    return _rmsnorm(xf, g_final, F32, "norm_final").reshape(batch, seq, d)
```

```python
import functools

import jax
import jax.numpy as jnp
import numpy as np
from jax import lax
from jax.experimental import pallas as pl
from jax.experimental.pallas import tpu as pltpu

GRID_W = 64
ROPE_THETA = 10000.0
EPS = 1e-6

MLA_HEADS = 16
MLA_NOPE = 128
MLA_ROPE = 64
MLA_V = 128
MLA_QK_PAD = 256
Q_LORA = 1024
KV_LORA = 512
MLA_SCALE = (MLA_NOPE + MLA_ROPE) ** -0.5

GQA_Q_HEADS = 16
GQA_KV_HEADS = 4
GQA_HEAD_DIM = 128
GQA_SCALE = GQA_HEAD_DIM ** -0.5
LOG2_E = 1.4426950408889634

LANES = 128
V7X_VMEM_BYTES = 64 * 1024 * 1024
VMEM_CAP_BYTES = V7X_VMEM_BYTES - 6 * 1024 * 1024

BF16 = jnp.bfloat16
F32 = jnp.float32


def _nbytes(shape, dtype):
    return int(np.prod(shape)) * jnp.dtype(dtype).itemsize


def _params(semantics, pipelined_blocks, temporaries=()):
    est = 2 * sum(_nbytes(s, d) for s, d in pipelined_blocks)
    est += sum(_nbytes(s, d) for s, d in temporaries)
    limit = min(VMEM_CAP_BYTES, int(est * 1.25) + (4 << 20))
    return pltpu.CompilerParams(dimension_semantics=semantics, vmem_limit_bytes=limit)


def _rmsnorm_kernel(x_ref, g_ref, o_ref):
    x = x_ref[...]
    ms = jnp.mean(x * x, axis=-1, keepdims=True)
    o_ref[...] = (x * lax.rsqrt(ms + EPS) * g_ref[...]).astype(o_ref.dtype)


def _rmsnorm(x, g, out_dtype, name, block_rows=256):
    t, d = x.shape
    blocks = [((block_rows, d), F32), ((block_rows, d), out_dtype)]
    return pl.pallas_call(
        _rmsnorm_kernel,
        out_shape=jax.ShapeDtypeStruct((t, d), out_dtype),
        grid=(t // block_rows,),
        in_specs=[pl.BlockSpec((block_rows, d), lambda i: (i, 0)),
                  pl.BlockSpec((1, d), lambda i: (0, 0))],
        out_specs=pl.BlockSpec((block_rows, d), lambda i: (i, 0)),
        compiler_params=_params(("parallel",), blocks, [((block_rows, d), F32)] * 2),
        name=name,
    )(x, g.reshape(1, d))


def _dot(a, b):
    return jnp.dot(a, b.astype(BF16), preferred_element_type=F32)


def _mm_kernel(a_ref, b_ref, o_ref):
    o_ref[...] = _dot(a_ref[...], b_ref[...]).astype(o_ref.dtype)


def _matmul(a, b, out_dtype, bm, bn, name):
    m, k = a.shape
    n = b.shape[1]
    blocks = [((bm, k), a.dtype), ((k, bn), b.dtype), ((bm, bn), out_dtype)]
    return pl.pallas_call(
        _mm_kernel,
        out_shape=jax.ShapeDtypeStruct((m, n), out_dtype),
        grid=(m // bm, n // bn),
        in_specs=[pl.BlockSpec((bm, k), lambda i, j: (i, 0)),
                  pl.BlockSpec((k, bn), lambda i, j: (0, j))],
        out_specs=pl.BlockSpec((bm, bn), lambda i, j: (i, j)),
        compiler_params=_params(("parallel", "arbitrary"), blocks, [((bm, bn), F32), ((k, bn), BF16)]),
        name=name,
    )(a, b)


def _mm_residual_kernel(a_ref, b_ref, r_ref, o_ref):
    o_ref[...] = r_ref[...] + _dot(a_ref[...], b_ref[...])


def _matmul_residual(a, b, r, bm, bn, name, k_block_index=0, k_block=None):
    m = a.shape[0]
    k = b.shape[0] if k_block is None else k_block
    n = b.shape[1]
    blocks = [((bm, k), a.dtype), ((k, bn), b.dtype), ((bm, bn), F32), ((bm, bn), F32)]
    return pl.pallas_call(
        _mm_residual_kernel,
        out_shape=jax.ShapeDtypeStruct((m, n), F32),
        grid=(m // bm, n // bn),
        in_specs=[pl.BlockSpec((bm, k), lambda i, j: (i, k_block_index)),
                  pl.BlockSpec((k, bn), lambda i, j: (k_block_index, j)),
                  pl.BlockSpec((bm, bn), lambda i, j: (i, j))],
        out_specs=pl.BlockSpec((bm, bn), lambda i, j: (i, j)),
        compiler_params=_params(("parallel", "arbitrary"), blocks, [((bm, bn), F32), ((k, bn), BF16)]),
        name=name,
    )(a, b, r)


def _sigmoid(x):
    return 1.0 / (1.0 + jnp.exp(-x))


def _merge_kernel(oa_ref, ob_ref, wa_ref, wb_ref, ga_ref, gb_ref, o_ref):
    pa = _dot(oa_ref[...], wa_ref[...])
    pb = _dot(ob_ref[...], wb_ref[...])
    o_ref[...] = (_sigmoid(ga_ref[...]) * pa + _sigmoid(gb_ref[...]) * pb).astype(o_ref.dtype)


def _gated_merge(o_a, o_b, w_a, w_b, z_gate, bm, bn, name):
    t, ka = o_a.shape
    kb = o_b.shape[1]
    n = w_a.shape[1]
    nb = n // bn
    blocks = [((bm, ka), BF16), ((bm, kb), BF16), ((ka, bn), w_a.dtype), ((kb, bn), w_b.dtype),
              ((bm, bn), F32), ((bm, bn), F32), ((bm, bn), BF16)]
    temps = [((bm, bn), F32)] * 3 + [((ka, bn), BF16), ((kb, bn), BF16)]
    return pl.pallas_call(
        _merge_kernel,
        out_shape=jax.ShapeDtypeStruct((t, n), BF16),
        grid=(t // bm, nb),
        in_specs=[pl.BlockSpec((bm, ka), lambda i, j: (i, 0)),
                  pl.BlockSpec((bm, kb), lambda i, j: (i, 0)),
                  pl.BlockSpec((ka, bn), lambda i, j: (0, j)),
                  pl.BlockSpec((kb, bn), lambda i, j: (0, j)),
                  pl.BlockSpec((bm, bn), lambda i, j: (i, j)),
                  pl.BlockSpec((bm, bn), lambda i, j: (i, j + nb))],
        out_specs=pl.BlockSpec((bm, bn), lambda i, j: (i, j)),
        compiler_params=_params(("parallel", "arbitrary"), blocks, temps),
        name=name,
    )(o_a, o_b, w_a, w_b, z_gate, z_gate)


def _swiglu_kernel(a_ref, wg_ref, wu_ref, o_ref):
    a = a_ref[...]
    g = _dot(a, wg_ref[...])
    u = _dot(a, wu_ref[...])
    o_ref[...] = (g * _sigmoid(g) * u).astype(o_ref.dtype)


def _swiglu(a, w_gate, w_up, bm, bn, name):
    t, k = a.shape
    n = w_gate.shape[1]
    blocks = [((bm, k), BF16), ((k, bn), w_gate.dtype), ((k, bn), w_up.dtype), ((bm, bn), BF16)]
    temps = [((bm, bn), F32)] * 3 + [((k, bn), BF16)] * 2
    return pl.pallas_call(
        _swiglu_kernel,
        out_shape=jax.ShapeDtypeStruct((t, n), BF16),
        grid=(t // bm, n // bn),
        in_specs=[pl.BlockSpec((bm, k), lambda i, j: (i, 0)),
                  pl.BlockSpec((k, bn), lambda i, j: (0, j)),
                  pl.BlockSpec((k, bn), lambda i, j: (0, j))],
        out_specs=pl.BlockSpec((bm, bn), lambda i, j: (i, j)),
        compiler_params=_params(("parallel", "arbitrary"), blocks, temps),
        name=name,
    )(a, w_gate, w_up)


def _rope_table(pos, dim):
    inv = ROPE_THETA ** (-jnp.arange(0, dim, 2, dtype=F32) / dim)
    ang = pos.astype(F32)[:, None] * inv[None, :]
    ang = jnp.concatenate([ang, ang], axis=-1)
    return jnp.cos(ang), jnp.sin(ang)


def _axial_tables(seq, rot_dim):
    rows = seq // GRID_W
    row_idx = jnp.repeat(jnp.arange(rows, dtype=jnp.int32), GRID_W)
    col_idx = jnp.tile(jnp.arange(GRID_W, dtype=jnp.int32), rows)
    half = rot_dim // 2
    cos_r, sin_r = _rope_table(row_idx, half)
    cos_c, sin_c = _rope_table(col_idx, half)
    cos = jnp.concatenate([cos_r, cos_c], axis=-1)
    sin = jnp.concatenate([sin_r, sin_c], axis=-1)
    first = (jnp.arange(rot_dim) % half) < (half // 2)
    sin_up = jnp.where(first[None, :], -sin, 0.0)
    sin_dn = jnp.where(first[None, :], 0.0, sin)
    pad = ((0, 0), (0, LANES - rot_dim))
    return tuple(jnp.pad(t, pad) for t in (cos, sin_up, sin_dn))


def _rope_lanes(x, cos, sin_up, sin_dn, quarter):
    up = pltpu.roll(x, LANES - quarter, 1)
    dn = pltpu.roll(x, quarter, 1)
    return x * cos + up * sin_up + dn * sin_dn


def _mla_q_kernel(z_ref, g_ref, w_ref, cos_ref, su_ref, sd_ref, o_ref, cq_ref, *, heads_per_step):
    @pl.when(pl.program_id(1) == 0)
    def _():
        z = z_ref[...]
        ms = jnp.mean(z * z, axis=-1, keepdims=True)
        cq_ref[...] = (z * lax.rsqrt(ms + EPS) * g_ref[...]).astype(cq_ref.dtype)

    q = _dot(cq_ref[...], w_ref[...]) * (MLA_SCALE * LOG2_E)
    cos, su, sd = cos_ref[...], su_ref[...], sd_ref[...]
    for h in range(heads_per_step):
        base = h * MLA_QK_PAD
        o_ref[:, base:base + MLA_NOPE] = q[:, base:base + MLA_NOPE].astype(o_ref.dtype)
        pe = q[:, base + MLA_NOPE:base + MLA_QK_PAD]
        o_ref[:, base + MLA_NOPE:base + MLA_QK_PAD] = _rope_lanes(
            pe, cos, su, sd, MLA_ROPE // 4).astype(o_ref.dtype)


def _mla_q_proj(z1, g_q_a, w_q, tables, seq, bm, heads_per_step, name):
    t = z1.shape[0]
    bn = heads_per_step * MLA_QK_PAD
    n = w_q.shape[1]
    s_blocks = seq // bm
    tab_spec = pl.BlockSpec((bm, LANES), lambda i, j: (i % s_blocks, 0))
    blocks = [((bm, Q_LORA), F32), ((Q_LORA, bn), BF16), ((bm, bn), BF16)] + [((bm, LANES), F32)] * 3
    return pl.pallas_call(
        functools.partial(_mla_q_kernel, heads_per_step=heads_per_step),
        out_shape=jax.ShapeDtypeStruct((t, n), BF16),
        grid=(t // bm, n // bn),
        in_specs=[pl.BlockSpec((bm, Q_LORA), lambda i, j: (i, 0)),
                  pl.BlockSpec((1, Q_LORA), lambda i, j: (0, 0)),
                  pl.BlockSpec((Q_LORA, bn), lambda i, j: (0, j)),
                  tab_spec, tab_spec, tab_spec],
        out_specs=pl.BlockSpec((bm, bn), lambda i, j: (i, j)),
        scratch_shapes=[pltpu.VMEM((bm, Q_LORA), BF16)],
        compiler_params=_params(("parallel", "arbitrary"), blocks,
                                [((bm, Q_LORA), BF16), ((bm, bn), F32), ((bm, Q_LORA), F32)]),
        name=name,
    )(z1, g_q_a.reshape(1, Q_LORA), w_q, *tables)


def _mla_kv_kernel(z_ref, pe_ref, g_ref, w_ref, cos_ref, su_ref, sd_ref, k_ref, v_ref, ckv_ref, kpe_ref,
                   *, heads_per_step):
    @pl.when(pl.program_id(1) == 0)
    def _():
        z = z_ref[...]
        ms = jnp.mean(z * z, axis=-1, keepdims=True)
        ckv_ref[...] = (z * lax.rsqrt(ms + EPS) * g_ref[...]).astype(ckv_ref.dtype)
        kpe_ref[...] = _rope_lanes(pe_ref[...], cos_ref[...], su_ref[...], sd_ref[...],
                                   MLA_ROPE // 4).astype(kpe_ref.dtype)

    kv = _dot(ckv_ref[...], w_ref[...])
    for h in range(heads_per_step):
        src = h * (MLA_NOPE + MLA_V)
        dst = h * MLA_QK_PAD
        k_ref[:, dst:dst + MLA_NOPE] = kv[:, src:src + MLA_NOPE].astype(k_ref.dtype)
        k_ref[:, dst + MLA_NOPE:dst + MLA_QK_PAD] = kpe_ref[...]
        v_ref[:, h * MLA_V:(h + 1) * MLA_V] = kv[:, src + MLA_NOPE:src + MLA_NOPE + MLA_V].astype(v_ref.dtype)


def _mla_kv_proj(z1, g_kv_a, w_kv, tables, seq, bm, heads_per_step, name):
    t = z1.shape[0]
    bn = heads_per_step * (MLA_NOPE + MLA_V)
    steps = w_kv.shape[1] // bn
    s_blocks = seq // bm
    tab_spec = pl.BlockSpec((bm, LANES), lambda i, j: (i % s_blocks, 0))
    kva_block = Q_LORA // KV_LORA
    kpe_block = (Q_LORA + KV_LORA) // LANES
    bk, bv = heads_per_step * MLA_QK_PAD, heads_per_step * MLA_V
    blocks = ([((bm, KV_LORA), F32), ((bm, LANES), F32), ((KV_LORA, bn), BF16), ((bm, bk), BF16),
               ((bm, bv), BF16)] + [((bm, LANES), F32)] * 3)
    return pl.pallas_call(
        functools.partial(_mla_kv_kernel, heads_per_step=heads_per_step),
        out_shape=(jax.ShapeDtypeStruct((t, steps * bk), BF16), jax.ShapeDtypeStruct((t, steps * bv), BF16)),
        grid=(t // bm, steps),
        in_specs=[pl.BlockSpec((bm, KV_LORA), lambda i, j: (i, kva_block)),
                  pl.BlockSpec((bm, LANES), lambda i, j: (i, kpe_block)),
                  pl.BlockSpec((1, KV_LORA), lambda i, j: (0, 0)),
                  pl.BlockSpec((KV_LORA, bn), lambda i, j: (0, j)),
                  tab_spec, tab_spec, tab_spec],
        out_specs=(pl.BlockSpec((bm, bk), lambda i, j: (i, j)),
                   pl.BlockSpec((bm, bv), lambda i, j: (i, j))),
        scratch_shapes=[pltpu.VMEM((bm, KV_LORA), BF16), pltpu.VMEM((bm, LANES), BF16)],
        compiler_params=_params(("parallel", "arbitrary"), blocks,
                                [((bm, KV_LORA), BF16), ((bm, bn), F32), ((bm, KV_LORA), F32)]),
        name=name,
    )(z1, z1, g_kv_a.reshape(1, KV_LORA), w_kv, *tables)


def _gqa_prep_kernel(z_ref, g_ref, c_ref, cos_ref, su_ref, sd_ref, o_ref, *, heads_per_step):
    cos, su, sd = cos_ref[...], su_ref[...], sd_ref[...]
    for h in range(heads_per_step):
        sl = slice(h * GQA_HEAD_DIM, (h + 1) * GQA_HEAD_DIM)
        x = z_ref[:, sl]
        ms = jnp.mean(x * x, axis=-1, keepdims=True)
        y = x * lax.rsqrt(ms + EPS) * g_ref[:, sl]
        o_ref[:, sl] = (_rope_lanes(y, cos, su, sd, GQA_HEAD_DIM // 4) * c_ref[:, sl]).astype(o_ref.dtype)


def _gqa_prep(z2, gains, post_scale, tables, seq, n_heads, br, heads_per_step, name):
    t = z2.shape[0]
    bn = heads_per_step * GQA_HEAD_DIM
    s_blocks = seq // br
    tab_spec = pl.BlockSpec((br, LANES), lambda i, j: (i % s_blocks, 0))
    row_spec = pl.BlockSpec((1, bn), lambda i, j: (0, j))
    blocks = [((br, bn), F32), ((br, bn), BF16)] + [((br, LANES), F32)] * 3
    return pl.pallas_call(
        functools.partial(_gqa_prep_kernel, heads_per_step=heads_per_step),
        out_shape=jax.ShapeDtypeStruct((t, n_heads * GQA_HEAD_DIM), BF16),
        grid=(t // br, n_heads // heads_per_step),
        in_specs=[pl.BlockSpec((br, bn), lambda i, j: (i, j)), row_spec, row_spec,
                  tab_spec, tab_spec, tab_spec],
        out_specs=pl.BlockSpec((br, bn), lambda i, j: (i, j)),
        compiler_params=_params(("parallel", "arbitrary"), blocks, [((br, bn), F32)] * 2),
        name=name,
    )(z2, gains, post_scale, *tables)


def _attention_kernel(q_ref, k_ref, v_ref, o_ref, *, bq):
    k = k_ref[...]
    v = v_ref[...]
    for qi in range(q_ref.shape[0] // bq):
        rows = slice(qi * bq, (qi + 1) * bq)
        s = lax.dot_general(q_ref[rows, :], k, (((1,), (1,)), ((), ())),
                            preferred_element_type=F32)
        m = jnp.max(s, axis=-1, keepdims=True)
        p = jnp.exp2(s - m)
        l = jnp.sum(p, axis=-1, keepdims=True)
        o = jnp.dot(p.astype(BF16), v, preferred_element_type=F32)
        o_ref[rows, :] = (o / l).astype(o_ref.dtype)


def _attention(q, k, v, *, n_q_heads, group, dk, dv, k_head_offset, bq, name):
    b, s, _ = q.shape
    n_blk = s // bq
    blocks = [((s, dk), BF16), ((s, dk), BF16), ((s, dv), BF16), ((s, dv), BF16)]
    return pl.pallas_call(
        functools.partial(_attention_kernel, bq=bq),
        out_shape=jax.ShapeDtypeStruct((b, s, n_q_heads * dv), BF16),
        grid=(b, n_q_heads),
        in_specs=[pl.BlockSpec((None, s, dk), lambda bi, h: (bi, 0, h)),
                  pl.BlockSpec((None, s, dk), lambda bi, h: (bi, 0, k_head_offset + h // group)),
                  pl.BlockSpec((None, s, dv), lambda bi, h: (bi, 0, h // group))],
        out_specs=pl.BlockSpec((None, s, dv), lambda bi, h: (bi, 0, h)),
        compiler_params=_params(("parallel", "arbitrary"), blocks,
                                [((bq, s), F32)] * (2 * n_blk) + [((bq, s), BF16)] * n_blk),
        name=name,
    )(q, k, v)


def _layer(xf, batch, seq, g_attn, w_in, g_q_a, w_q_b, g_kv_a, w_kv_b, g_qn, g_kn,
           w_branch_a, w_branch_b, w_o, g_ffn, w_gate, w_up, w_down):
    t, d = xf.shape
    seg1 = Q_LORA + KV_LORA + MLA_ROPE
    seg2 = (GQA_Q_HEADS + 2 * GQA_KV_HEADS) * GQA_HEAD_DIM

    w1 = jnp.pad(w_in[:, :seg1], ((0, 0), (0, LANES - MLA_ROPE))).astype(BF16)
    w2 = w_in[:, seg1:seg1 + seg2].astype(BF16)
    w3 = w_in[:, seg1 + seg2:].astype(BF16)
    w_q = jnp.pad(w_q_b.reshape(Q_LORA, MLA_HEADS, MLA_NOPE + MLA_ROPE),
                  ((0, 0), (0, 0), (0, MLA_QK_PAD - MLA_NOPE - MLA_ROPE)))
    w_q = w_q.reshape(Q_LORA, MLA_HEADS * MLA_QK_PAD).astype(BF16)
    w_kv = w_kv_b.astype(BF16)
    gqa_gains = jnp.concatenate([jnp.tile(g_qn, GQA_Q_HEADS), jnp.tile(g_kn, GQA_KV_HEADS)]).reshape(1, -1)
    gqa_post = jnp.concatenate([jnp.full((GQA_Q_HEADS * GQA_HEAD_DIM,), GQA_SCALE * LOG2_E, F32),
                                jnp.ones((GQA_KV_HEADS * GQA_HEAD_DIM,), F32)]).reshape(1, -1)
    mla_tables = _axial_tables(seq, MLA_ROPE)
    gqa_tables = _axial_tables(seq, GQA_HEAD_DIM)

    h = _rmsnorm(xf, g_attn, BF16, "norm_attn")
    z1 = _matmul(h, w1, F32, 512, w1.shape[1], "in_proj_latent")
    z2 = _matmul(h, w2, F32, 1024, 1024, "in_proj_gqa")
    z_gate = _matmul(h, w3, F32, 1024, 1024, "in_proj_gate")

    q_a = _mla_q_proj(z1, g_q_a, w_q, mla_tables, seq, 1024, 2, "mla_q_proj")
    k_a, v_a = _mla_kv_proj(z1, g_kv_a, w_kv, mla_tables, seq, 1024, 2, "mla_kv_proj")
    o_a = _attention(q_a.reshape(batch, seq, -1), k_a.reshape(batch, seq, -1), v_a.reshape(batch, seq, -1),
                     n_q_heads=MLA_HEADS, group=1, dk=MLA_QK_PAD, dv=MLA_V, k_head_offset=0,
                     bq=512, name="mla_attention")

    qk_b = _gqa_prep(z2, gqa_gains, gqa_post, gqa_tables, seq, GQA_Q_HEADS + GQA_KV_HEADS, 512, 4,
                     "gqa_prep")
    v_b = z2[:, (GQA_Q_HEADS + GQA_KV_HEADS) * GQA_HEAD_DIM:].astype(BF16)
    qk_b3 = qk_b.reshape(batch, seq, -1)
    o_b = _attention(qk_b3, qk_b3, v_b.reshape(batch, seq, -1),
                     n_q_heads=GQA_Q_HEADS, group=GQA_Q_HEADS // GQA_KV_HEADS, dk=GQA_HEAD_DIM,
                     dv=GQA_HEAD_DIM, k_head_offset=GQA_Q_HEADS, bq=512, name="gqa_attention")

    m = _gated_merge(o_a.reshape(t, -1), o_b.reshape(t, -1), w_branch_a, w_branch_b, z_gate,
                     1024, 512, "gated_merge")
    x1 = _matmul_residual(m, w_o, xf, 1024, 512, "out_proj")

    h2 = _rmsnorm(x1, g_ffn, BF16, "norm_ffn")
    act = _swiglu(h2, w_gate, w_up, 1024, 256, "ffn_gate_up")
    k_half = act.shape[1] // 2
    wd = w_down.astype(BF16)
    y = _matmul_residual(act, wd, x1, 1024, 512, "ffn_down_lo", k_block_index=0, k_block=k_half)
    return _matmul_residual(act, wd, y, 1024, 512, "ffn_down_hi", k_block_index=1, k_block=k_half)


def kernel(x, g_attn, w_in, g_q_a, w_q_b, g_kv_a, w_kv_b, g_qn, g_kn, w_branch_a, w_branch_b, w_o,
           g_ffn, w_gate, w_up, w_down, g_final):
    batch, seq, d = x.shape
    xf = x.reshape(batch * seq, d)
    for l in range(g_attn.shape[0]):
        xf = _layer(xf, batch, seq, g_attn[l], w_in[l], g_q_a[l], w_q_b[l], g_kv_a[l], w_kv_b[l],
                    g_qn[l], g_kn[l], w_branch_a[l], w_branch_b[l], w_o[l], g_ffn[l],
                    w_gate[l], w_up[l], w_down[l])
    return _rmsnorm(xf, g_final, F32, "norm_final").reshape(batch, seq, d)
```

```python
import functools

import jax
import jax.numpy as jnp
import numpy as np
from jax import lax
from jax.experimental import pallas as pl
from jax.experimental.pallas import tpu as pltpu

GRID_W = 64
ROPE_THETA = 10000.0
EPS = 1e-6

MLA_HEADS = 16
MLA_NOPE = 128
MLA_ROPE = 64
MLA_V = 128
MLA_QK_PAD = 256
Q_LORA = 1024
KV_LORA = 512
MLA_SCALE = (MLA_NOPE + MLA_ROPE) ** -0.5

GQA_Q_HEADS = 16
GQA_KV_HEADS = 4
GQA_HEAD_DIM = 128
GQA_SCALE = GQA_HEAD_DIM ** -0.5
LOG2_E = 1.4426950408889634

LANES = 128
V7X_VMEM_BYTES = 64 * 1024 * 1024
VMEM_CAP_BYTES = V7X_VMEM_BYTES - 6 * 1024 * 1024

IN_BLOCK = 512
LATENT_COLS = Q_LORA + KV_LORA
IN_SHIFT = MLA_ROPE

BF16 = jnp.bfloat16
F32 = jnp.float32

_RESIDENT = pl.Buffered(1)
RESIDENT_MIN_BYTES = 16 * 1024 * 1024


def _nbytes(shape, dtype):
    return int(np.prod(shape)) * jnp.dtype(dtype).itemsize


def _params(semantics, pipelined_blocks, single_blocks=()):
    est = 2 * sum(_nbytes(s, d) for s, d in pipelined_blocks)
    est += sum(_nbytes(s, d) for s, d in single_blocks)
    limit = min(VMEM_CAP_BYTES, int(est * 1.25) + (4 << 20))
    return pltpu.CompilerParams(dimension_semantics=semantics, vmem_limit_bytes=limit)


def _dot(a, b):
    return jnp.dot(a, b.astype(BF16), preferred_element_type=F32)


def _sigmoid(x):
    return 1.0 / (1.0 + jnp.exp(-x))


def _rmsnorm_kernel(x_ref, g_ref, o_ref):
    x = x_ref[...]
    ms = jnp.mean(x * x, axis=-1, keepdims=True)
    o_ref[...] = (x * lax.rsqrt(ms + EPS) * g_ref[...]).astype(o_ref.dtype)


def _rmsnorm(x, g, out_dtype, name, block_rows=256):
    t, d = x.shape
    blocks = [((block_rows, d), F32), ((block_rows, d), out_dtype)]
    return pl.pallas_call(
        _rmsnorm_kernel,
        out_shape=jax.ShapeDtypeStruct((t, d), out_dtype),
        grid=(t // block_rows,),
        in_specs=[pl.BlockSpec((block_rows, d), lambda i: (i, 0)),
                  pl.BlockSpec((1, d), lambda i: (0, 0))],
        out_specs=pl.BlockSpec((block_rows, d), lambda i: (i, 0)),
        compiler_params=_params(("parallel",), blocks, [((block_rows, d), F32)] * 2),
        name=name,
    )(x, g.reshape(1, d))


def _rope_table(pos, dim):
    inv = ROPE_THETA ** (-jnp.arange(0, dim, 2, dtype=F32) / dim)
    ang = pos.astype(F32)[:, None] * inv[None, :]
    ang = jnp.concatenate([ang, ang], axis=-1)
    return jnp.cos(ang), jnp.sin(ang)


def _axial_tables(seq, rot_dim):
    rows = seq // GRID_W
    row_idx = jnp.repeat(jnp.arange(rows, dtype=jnp.int32), GRID_W)
    col_idx = jnp.tile(jnp.arange(GRID_W, dtype=jnp.int32), rows)
    half = rot_dim // 2
    cos_r, sin_r = _rope_table(row_idx, half)
    cos_c, sin_c = _rope_table(col_idx, half)
    cos = jnp.concatenate([cos_r, cos_c], axis=-1)
    sin = jnp.concatenate([sin_r, sin_c], axis=-1)
    first = (jnp.arange(rot_dim) % half) < (half // 2)
    sin_up = jnp.where(first[None, :], -sin, 0.0)
    sin_dn = jnp.where(first[None, :], 0.0, sin)
    pad = ((0, 0), (0, LANES - rot_dim))
    return tuple(jnp.pad(t, pad) for t in (cos, sin_up, sin_dn))


def _rope_lanes(x, cos, sin_up, sin_dn, quarter):
    up = pltpu.roll(x, LANES - quarter, 1)
    dn = pltpu.roll(x, quarter, 1)
    return x * cos + up * sin_up + dn * sin_dn


def _in_proj_latent_kernel(h_ref, w_ref, wk_ref, z_ref, zk_ref):
    h = h_ref[...]
    z_ref[...] = _dot(h, w_ref[...])

    @pl.when(pl.program_id(1) == 0)
    def _():
        lane = lax.broadcasted_iota(jnp.int32, (1, LANES), 1)
        zk_ref[...] = jnp.where(lane < MLA_ROPE, _dot(h, wk_ref[...]), 0.0)


def _in_proj_latent(h, w_in, bm, name):
    t, k = h.shape
    nb = LATENT_COLS // IN_BLOCK
    blocks = [((bm, k), BF16), ((k, IN_BLOCK), F32), ((k, LANES), F32), ((bm, IN_BLOCK), F32),
              ((bm, LANES), F32)]
    return pl.pallas_call(
        _in_proj_latent_kernel,
        out_shape=(jax.ShapeDtypeStruct((t, LATENT_COLS), F32), jax.ShapeDtypeStruct((t, LANES), F32)),
        grid=(t // bm, nb),
        in_specs=[pl.BlockSpec((bm, k), lambda i, j: (i, 0)),
                  pl.BlockSpec((k, IN_BLOCK), lambda i, j: (0, j)),
                  pl.BlockSpec((k, LANES), lambda i, j: (0, LATENT_COLS // LANES))],
        out_specs=(pl.BlockSpec((bm, IN_BLOCK), lambda i, j: (i, j)),
                   pl.BlockSpec((bm, LANES), lambda i, j: (i, 0))),
        compiler_params=_params(("parallel", "arbitrary"), blocks, [((bm, IN_BLOCK), F32)]),
        name=name,
    )(h, w_in, w_in)


def _shifted_weight_block(wm_ref, wx_ref):
    lo = lax.broadcasted_iota(jnp.int32, (1, LANES), 1) < IN_SHIFT
    n_pieces = IN_BLOCK // LANES
    pieces = [wm_ref[:, c * LANES:(c + 1) * LANES] for c in range(n_pieces)] + [wx_ref[...]]
    rolled = [pltpu.roll(p, LANES - IN_SHIFT, 1) for p in pieces]
    return jnp.concatenate([jnp.where(lo, rolled[c], rolled[c + 1]).astype(BF16) for c in range(n_pieces)],
                           axis=1)


def _in_proj_cast_kernel(h_ref, wm_ref, wx_ref, o_ref):
    acc = jnp.dot(h_ref[...], _shifted_weight_block(wm_ref, wx_ref), preferred_element_type=F32)
    o_ref[...] = acc.astype(o_ref.dtype)


def _in_proj_sigmoid_kernel(h_ref, wm_ref, wx_ref, o_ref):
    acc = jnp.dot(h_ref[...], _shifted_weight_block(wm_ref, wx_ref), preferred_element_type=F32)
    o_ref[...] = _sigmoid(acc).astype(o_ref.dtype)


def _in_proj_shifted(body, h, w_in, first_col, n_cols, out_dtype, bm, name):
    t, k = h.shape
    first = (first_col - IN_SHIFT) // IN_BLOCK
    assert first * IN_BLOCK + IN_SHIFT == first_col and n_cols % IN_BLOCK == 0
    resident_h = _nbytes((bm, k), BF16) >= RESIDENT_MIN_BYTES
    pipelined = [((k, IN_BLOCK), F32), ((k, LANES), F32), ((bm, IN_BLOCK), out_dtype)]
    pipelined += [] if resident_h else [((bm, k), BF16)]
    single = [((bm, IN_BLOCK), F32), ((k, IN_BLOCK), BF16)] + ([((bm, k), BF16)] if resident_h else [])
    return pl.pallas_call(
        body,
        out_shape=jax.ShapeDtypeStruct((t, n_cols), out_dtype),
        grid=(t // bm, n_cols // IN_BLOCK),
        in_specs=[pl.BlockSpec((bm, k), lambda i, j: (i, 0), pipeline_mode=_RESIDENT if resident_h else None),
                  pl.BlockSpec((k, IN_BLOCK), lambda i, j: (0, first + j)),
                  pl.BlockSpec((k, LANES), lambda i, j: (0, (first + j + 1) * (IN_BLOCK // LANES)))],
        out_specs=pl.BlockSpec((bm, IN_BLOCK), lambda i, j: (i, j)),
        compiler_params=_params(("parallel", "arbitrary"), pipelined, single),
        name=name,
    )(h, w_in, w_in)


def _gqa_prep_kernel(z_ref, g_ref, c_ref, cos_ref, su_ref, sd_ref, o_ref, *, heads_per_step):
    cos, su, sd = cos_ref[...], su_ref[...], sd_ref[...]
    for h in range(heads_per_step):
        sl = slice(h * GQA_HEAD_DIM, (h + 1) * GQA_HEAD_DIM)
        x = z_ref[:, sl]
        ms = jnp.mean(x * x, axis=-1, keepdims=True)
        y = x * lax.rsqrt(ms + EPS) * g_ref[:, sl]
        o_ref[:, sl] = (_rope_lanes(y, cos, su, sd, GQA_HEAD_DIM // 4) * c_ref[:, sl]).astype(o_ref.dtype)


def _gqa_prep(z, gains, post_scale, tables, seq, br, heads_per_step, name):
    t, n = z.shape
    bn = heads_per_step * GQA_HEAD_DIM
    s_blocks = seq // br
    tab_spec = pl.BlockSpec((br, LANES), lambda i, j: (i % s_blocks, 0))
    row_spec = pl.BlockSpec((1, bn), lambda i, j: (0, j))
    blocks = [((br, bn), F32), ((br, bn), BF16)] + [((br, LANES), F32)] * 3
    return pl.pallas_call(
        functools.partial(_gqa_prep_kernel, heads_per_step=heads_per_step),
        out_shape=jax.ShapeDtypeStruct((t, n), BF16),
        grid=(t // br, n // bn),
        in_specs=[pl.BlockSpec((br, bn), lambda i, j: (i, j)), row_spec, row_spec,
                  tab_spec, tab_spec, tab_spec],
        out_specs=pl.BlockSpec((br, bn), lambda i, j: (i, j)),
        compiler_params=_params(("parallel", "arbitrary"), blocks, [((br, bn), F32)] * 2),
        name=name,
    )(z, gains, post_scale, *tables)


def _mla_q_kernel(z_ref, g_ref, w_ref, cos_ref, su_ref, sd_ref, o_ref, cq_ref, *, heads_per_step):
    @pl.when(pl.program_id(1) == 0)
    def _():
        z = z_ref[...]
        ms = jnp.mean(z * z, axis=-1, keepdims=True)
        cq_ref[...] = (z * lax.rsqrt(ms + EPS) * g_ref[...]).astype(cq_ref.dtype)

    q = _dot(cq_ref[...], w_ref[...]) * (MLA_SCALE * LOG2_E)
    cos, su, sd = cos_ref[...], su_ref[...], sd_ref[...]
    for h in range(heads_per_step):
        base = h * MLA_QK_PAD
        o_ref[:, base:base + MLA_NOPE] = q[:, base:base + MLA_NOPE].astype(o_ref.dtype)
        pe = q[:, base + MLA_NOPE:base + MLA_QK_PAD]
        o_ref[:, base + MLA_NOPE:base + MLA_QK_PAD] = _rope_lanes(
            pe, cos, su, sd, MLA_ROPE // 4).astype(o_ref.dtype)


def _mla_q_proj(z_lat, g_q_a, w_q, tables, seq, bm, heads_per_step, name):
    t = z_lat.shape[0]
    bn = heads_per_step * MLA_QK_PAD
    n = w_q.shape[1]
    s_blocks = seq // bm
    tab_spec = pl.BlockSpec((bm, LANES), lambda i, j: (i % s_blocks, 0))
    blocks = [((bm, Q_LORA), F32), ((Q_LORA, bn), BF16), ((bm, bn), BF16)] + [((bm, LANES), F32)] * 3
    return pl.pallas_call(
        functools.partial(_mla_q_kernel, heads_per_step=heads_per_step),
        out_shape=jax.ShapeDtypeStruct((t, n), BF16),
        grid=(t // bm, n // bn),
        in_specs=[pl.BlockSpec((bm, Q_LORA), lambda i, j: (i, 0)),
                  pl.BlockSpec((1, Q_LORA), lambda i, j: (0, 0)),
                  pl.BlockSpec((Q_LORA, bn), lambda i, j: (0, j)),
                  tab_spec, tab_spec, tab_spec],
        out_specs=pl.BlockSpec((bm, bn), lambda i, j: (i, j)),
        scratch_shapes=[pltpu.VMEM((bm, Q_LORA), BF16)],
        compiler_params=_params(("parallel", "arbitrary"), blocks,
                                [((bm, Q_LORA), BF16), ((bm, bn), F32), ((bm, Q_LORA), F32)]),
        name=name,
    )(z_lat, g_q_a.reshape(1, Q_LORA), w_q, *tables)


def _mla_kv_kernel(z_ref, pe_ref, g_ref, w_ref, cos_ref, su_ref, sd_ref, k_ref, v_ref, ckv_ref, kpe_ref,
                   *, heads_per_step):
    @pl.when(pl.program_id(1) == 0)
    def _():
        z = z_ref[...]
        ms = jnp.mean(z * z, axis=-1, keepdims=True)
        ckv_ref[...] = (z * lax.rsqrt(ms + EPS) * g_ref[...]).astype(ckv_ref.dtype)
        kpe_ref[...] = _rope_lanes(pe_ref[...], cos_ref[...], su_ref[...], sd_ref[...],
                                   MLA_ROPE // 4).astype(kpe_ref.dtype)

    kv = _dot(ckv_ref[...], w_ref[...])
    for h in range(heads_per_step):
        src = h * (MLA_NOPE + MLA_V)
        dst = h * MLA_QK_PAD
        k_ref[:, dst:dst + MLA_NOPE] = kv[:, src:src + MLA_NOPE].astype(k_ref.dtype)
        k_ref[:, dst + MLA_NOPE:dst + MLA_QK_PAD] = kpe_ref[...]
        v_ref[:, h * MLA_V:(h + 1) * MLA_V] = kv[:, src + MLA_NOPE:src + MLA_NOPE + MLA_V].astype(v_ref.dtype)


def _mla_kv_proj(z_lat, z_kpe, g_kv_a, w_kv, tables, seq, bm, heads_per_step, name):
    t = z_lat.shape[0]
    bn = heads_per_step * (MLA_NOPE + MLA_V)
    steps = w_kv.shape[1] // bn
    s_blocks = seq // bm
    tab_spec = pl.BlockSpec((bm, LANES), lambda i, j: (i % s_blocks, 0))
    kva_block = Q_LORA // KV_LORA
    bk, bv = heads_per_step * MLA_QK_PAD, heads_per_step * MLA_V
    blocks = ([((bm, KV_LORA), F32), ((bm, LANES), F32), ((KV_LORA, bn), BF16), ((bm, bk), BF16),
               ((bm, bv), BF16)] + [((bm, LANES), F32)] * 3)
    return pl.pallas_call(
        functools.partial(_mla_kv_kernel, heads_per_step=heads_per_step),
        out_shape=(jax.ShapeDtypeStruct((t, steps * bk), BF16), jax.ShapeDtypeStruct((t, steps * bv), BF16)),
        grid=(t // bm, steps),
        in_specs=[pl.BlockSpec((bm, KV_LORA), lambda i, j: (i, kva_block)),
                  pl.BlockSpec((bm, LANES), lambda i, j: (i, 0)),
                  pl.BlockSpec((1, KV_LORA), lambda i, j: (0, 0)),
                  pl.BlockSpec((KV_LORA, bn), lambda i, j: (0, j)),
                  tab_spec, tab_spec, tab_spec],
        out_specs=(pl.BlockSpec((bm, bk), lambda i, j: (i, j)),
                   pl.BlockSpec((bm, bv), lambda i, j: (i, j))),
        scratch_shapes=[pltpu.VMEM((bm, KV_LORA), BF16), pltpu.VMEM((bm, LANES), BF16)],
        compiler_params=_params(("parallel", "arbitrary"), blocks,
                                [((bm, KV_LORA), BF16), ((bm, bn), F32), ((bm, KV_LORA), F32)]),
        name=name,
    )(z_lat, z_kpe, g_kv_a.reshape(1, KV_LORA), w_kv, *tables)


def _attention_kernel(q_ref, k_ref, v_ref, o_ref, *, bq):
    k = k_ref[...]
    v = v_ref[...]
    for qi in range(q_ref.shape[0] // bq):
        rows = slice(qi * bq, (qi + 1) * bq)
        s = lax.dot_general(q_ref[rows, :], k, (((1,), (1,)), ((), ())),
                            preferred_element_type=F32)
        m = jnp.max(s, axis=-1, keepdims=True)
        p = jnp.exp2(s - m)
        l = jnp.sum(p, axis=-1, keepdims=True)
        o = jnp.dot(p.astype(BF16), v, preferred_element_type=F32)
        o_ref[rows, :] = (o / l).astype(o_ref.dtype)


def _attention(q, k, v, *, n_q_heads, group, dk, dv, k_head_offset, v_head_offset, bq, name):
    b, s, _ = q.shape
    n_blk = s // bq
    blocks = [((s, dk), BF16), ((s, dk), BF16), ((s, dv), BF16), ((s, dv), BF16)]
    return pl.pallas_call(
        functools.partial(_attention_kernel, bq=bq),
        out_shape=jax.ShapeDtypeStruct((b, s, n_q_heads * dv), BF16),
        grid=(b, n_q_heads),
        in_specs=[pl.BlockSpec((None, s, dk), lambda bi, h: (bi, 0, h)),
                  pl.BlockSpec((None, s, dk), lambda bi, h: (bi, 0, k_head_offset + h // group)),
                  pl.BlockSpec((None, s, dv), lambda bi, h: (bi, 0, v_head_offset + h // group))],
        out_specs=pl.BlockSpec((None, s, dv), lambda bi, h: (bi, 0, h)),
        compiler_params=_params(("parallel", "arbitrary"), blocks,
                                [((bq, s), F32)] * (2 * n_blk) + [((bq, s), BF16)] * n_blk),
        name=name,
    )(q, k, v)


def _merge_kernel(oa_ref, ob_ref, wa_ref, wb_ref, ga_ref, gb_ref, o_ref):
    pa = _dot(oa_ref[...], wa_ref[...])
    pb = _dot(ob_ref[...], wb_ref[...])
    o_ref[...] = (ga_ref[...].astype(F32) * pa + gb_ref[...].astype(F32) * pb).astype(o_ref.dtype)


def _gated_merge(o_a, o_b, w_a, w_b, gates, gate_block_offset, bm, bn, name):
    t, ka = o_a.shape
    kb = o_b.shape[1]
    n = w_a.shape[1]
    nb = n // bn
    pipelined = [((ka, bn), w_a.dtype), ((kb, bn), w_b.dtype), ((bm, bn), BF16), ((bm, bn), BF16),
                 ((bm, bn), BF16)]
    single = [((bm, ka), BF16), ((bm, kb), BF16)] + [((bm, bn), F32)] * 2
    return pl.pallas_call(
        _merge_kernel,
        out_shape=jax.ShapeDtypeStruct((t, n), BF16),
        grid=(t // bm, nb),
        in_specs=[pl.BlockSpec((bm, ka), lambda i, j: (i, 0), pipeline_mode=_RESIDENT),
                  pl.BlockSpec((bm, kb), lambda i, j: (i, 0), pipeline_mode=_RESIDENT),
                  pl.BlockSpec((ka, bn), lambda i, j: (0, j)),
                  pl.BlockSpec((kb, bn), lambda i, j: (0, j)),
                  pl.BlockSpec((bm, bn), lambda i, j: (i, gate_block_offset + j)),
                  pl.BlockSpec((bm, bn), lambda i, j: (i, gate_block_offset + nb + j))],
        out_specs=pl.BlockSpec((bm, bn), lambda i, j: (i, j)),
        compiler_params=_params(("parallel", "arbitrary"), pipelined, single),
        name=name,
    )(o_a, o_b, w_a, w_b, gates, gates)


def _mm_residual_kernel(a_ref, b_ref, r_ref, o_ref):
    o_ref[...] = r_ref[...] + _dot(a_ref[...], b_ref[...])


def _matmul_residual(a, b, r, bm, bn, name, k_block_index=0, k_block=None, resident_a=False):
    m = a.shape[0]
    k = b.shape[0] if k_block is None else k_block
    n = b.shape[1]
    a_block = ((bm, k), a.dtype)
    pipelined = [((k, bn), b.dtype), ((bm, bn), F32), ((bm, bn), F32)] + ([] if resident_a else [a_block])
    single = [((bm, bn), F32)] + ([a_block] if resident_a else [])
    return pl.pallas_call(
        _mm_residual_kernel,
        out_shape=jax.ShapeDtypeStruct((m, n), F32),
        grid=(m // bm, n // bn),
        in_specs=[pl.BlockSpec((bm, k), lambda i, j: (i, k_block_index),
                               pipeline_mode=_RESIDENT if resident_a else None),
                  pl.BlockSpec((k, bn), lambda i, j: (k_block_index, j)),
                  pl.BlockSpec((bm, bn), lambda i, j: (i, j))],
        out_specs=pl.BlockSpec((bm, bn), lambda i, j: (i, j)),
        compiler_params=_params(("parallel", "arbitrary"), pipelined, single),
        name=name,
    )(a, b, r)


def _swiglu_kernel(a_ref, wg_ref, wu_ref, o_ref):
    a = a_ref[...]
    g = _dot(a, wg_ref[...])
    u = _dot(a, wu_ref[...])
    o_ref[...] = (g * _sigmoid(g) * u).astype(o_ref.dtype)


def _swiglu(a, w_gate, w_up, bm, bn, name):
    t, k = a.shape
    n = w_gate.shape[1]
    pipelined = [((k, bn), w_gate.dtype), ((k, bn), w_up.dtype), ((bm, bn), BF16)]
    single = [((bm, k), BF16)] + [((bm, bn), F32)] * 3
    return pl.pallas_call(
        _swiglu_kernel,
        out_shape=jax.ShapeDtypeStruct((t, n), BF16),
        grid=(t // bm, n // bn),
        in_specs=[pl.BlockSpec((bm, k), lambda i, j: (i, 0), pipeline_mode=_RESIDENT),
                  pl.BlockSpec((k, bn), lambda i, j: (0, j)),
                  pl.BlockSpec((k, bn), lambda i, j: (0, j))],
        out_specs=pl.BlockSpec((bm, bn), lambda i, j: (i, j)),
        compiler_params=_params(("parallel", "arbitrary"), pipelined, single),
        name=name,
    )(a, w_gate, w_up)


def _layer(xf, batch, seq, g_attn, w_in, g_q_a, w_q_b, g_kv_a, w_kv_b, g_qn, g_kn,
           w_branch_a, w_branch_b, w_o, g_ffn, w_gate, w_up, w_down):
    t, d = xf.shape
    w_q = jnp.pad(w_q_b.reshape(Q_LORA, MLA_HEADS, MLA_NOPE + MLA_ROPE),
                  ((0, 0), (0, 0), (0, MLA_QK_PAD - MLA_NOPE - MLA_ROPE)))
    w_q = w_q.reshape(Q_LORA, MLA_HEADS * MLA_QK_PAD).astype(BF16)
    w_kv = w_kv_b.astype(BF16)
    gqa_gains = jnp.concatenate([jnp.tile(g_qn, GQA_Q_HEADS), jnp.tile(g_kn, GQA_KV_HEADS)]).reshape(1, -1)
    gqa_post = jnp.concatenate([jnp.full((GQA_Q_HEADS * GQA_HEAD_DIM,), GQA_SCALE * LOG2_E, F32),
                                jnp.ones((GQA_KV_HEADS * GQA_HEAD_DIM,), F32)]).reshape(1, -1)
    mla_tables = _axial_tables(seq, MLA_ROPE)
    gqa_tables = _axial_tables(seq, GQA_HEAD_DIM)

    h = _rmsnorm(xf, g_attn, BF16, "norm_attn")
    z_lat, z_kpe = _in_proj_latent(h, w_in, 1024, "in_proj_latent")
    col = LATENT_COLS + IN_SHIFT
    n_qk = (GQA_Q_HEADS + GQA_KV_HEADS) * GQA_HEAD_DIM
    n_v = GQA_KV_HEADS * GQA_HEAD_DIM
    z_qk = _in_proj_shifted(_in_proj_cast_kernel, h, w_in, col, n_qk, F32, 1024, "in_proj_qk")
    v_b = _in_proj_shifted(_in_proj_cast_kernel, h, w_in, col + n_qk, n_v, BF16, 1024, "in_proj_v")
    gates = _in_proj_shifted(_in_proj_sigmoid_kernel, h, w_in, col + n_qk + n_v, 2 * d, BF16, 2048,
                             "in_proj_gate")
    qk_b = _gqa_prep(z_qk, gqa_gains, gqa_post, gqa_tables, seq, 512, 4, "gqa_prep")

    q_a = _mla_q_proj(z_lat, g_q_a, w_q, mla_tables, seq, 1024, 4, "mla_q_proj")
    k_a, v_a = _mla_kv_proj(z_lat, z_kpe, g_kv_a, w_kv, mla_tables, seq, 1024, 2, "mla_kv_proj")
    o_a = _attention(q_a.reshape(batch, seq, -1), k_a.reshape(batch, seq, -1), v_a.reshape(batch, seq, -1),
                     n_q_heads=MLA_HEADS, group=1, dk=MLA_QK_PAD, dv=MLA_V, k_head_offset=0,
                     v_head_offset=0, bq=512, name="mla_attention")

    qk_b3 = qk_b.reshape(batch, seq, -1)
    o_b = _attention(qk_b3, qk_b3, v_b.reshape(batch, seq, -1), n_q_heads=GQA_Q_HEADS,
                     group=GQA_Q_HEADS // GQA_KV_HEADS, dk=GQA_HEAD_DIM, dv=GQA_HEAD_DIM,
                     k_head_offset=GQA_Q_HEADS, v_head_offset=0, bq=512, name="gqa_attention")

    m = _gated_merge(o_a.reshape(t, -1), o_b.reshape(t, -1), w_branch_a, w_branch_b, gates,
                     0, 2048, 512, "gated_merge")
    x1 = _matmul_residual(m, w_o, xf, 2048, 512, "out_proj", resident_a=True)

    h2 = _rmsnorm(x1, g_ffn, BF16, "norm_ffn")
    act = _swiglu(h2, w_gate, w_up, 2048, 256, "ffn_gate_up")
    k_half = act.shape[1] // 2
    wd = w_down.astype(BF16)
    y = _matmul_residual(act, wd, x1, 1024, 512, "ffn_down_lo", k_block_index=0, k_block=k_half)
    return _matmul_residual(act, wd, y, 1024, 512, "ffn_down_hi", k_block_index=1, k_block=k_half)


def kernel(x, g_attn, w_in, g_q_a, w_q_b, g_kv_a, w_kv_b, g_qn, g_kn, w_branch_a, w_branch_b, w_o,
           g_ffn, w_gate, w_up, w_down, g_final):
    batch, seq, d = x.shape
    xf = x.reshape(batch * seq, d)
    for l in range(g_attn.shape[0]):
        xf = _layer(xf, batch, seq, g_attn[l], w_in[l], g_q_a[l], w_q_b[l], g_kv_a[l], w_kv_b[l],
                    g_qn[l], g_kn[l], w_branch_a[l], w_branch_b[l], w_o[l], g_ffn[l],
                    w_gate[l], w_up[l], w_down[l])
    return _rmsnorm(xf, g_final, F32, "norm_final").reshape(batch, seq, d)
```

```python
import functools

import jax
import jax.numpy as jnp
import numpy as np
from jax import lax
from jax.experimental import pallas as pl
from jax.experimental.pallas import tpu as pltpu

GRID_W = 64
ROPE_THETA = 10000.0
EPS = 1e-6

MLA_HEADS = 16
MLA_NOPE = 128
MLA_ROPE = 64
MLA_V = 128
MLA_QK_PAD = 256
Q_LORA = 1024
KV_LORA = 512
MLA_SCALE = (MLA_NOPE + MLA_ROPE) ** -0.5

GQA_Q_HEADS = 16
GQA_KV_HEADS = 4
GQA_HEAD_DIM = 128
GQA_SCALE = GQA_HEAD_DIM ** -0.5
LOG2_E = 1.4426950408889634

LANES = 128
SUBLANES = 8
V7X_VMEM_BYTES = 64 * 1024 * 1024
VMEM_CAP_BYTES = V7X_VMEM_BYTES - 6 * 1024 * 1024

IN_BLOCK = 512
LATENT_COLS = Q_LORA + KV_LORA

BF16 = jnp.bfloat16
F32 = jnp.float32

_RESIDENT = pl.Buffered(1)
RESIDENT_MIN_BYTES = 16 * 1024 * 1024


def _nbytes(shape, dtype):
    return int(np.prod(shape)) * jnp.dtype(dtype).itemsize


def _params(semantics, pipelined_blocks, single_blocks=()):
    est = 2 * sum(_nbytes(s, d) for s, d in pipelined_blocks)
    est += sum(_nbytes(s, d) for s, d in single_blocks)
    limit = min(VMEM_CAP_BYTES, int(est * 1.25) + (4 << 20))
    return pltpu.CompilerParams(dimension_semantics=semantics, vmem_limit_bytes=limit)


def _dot(a, b):
    return jnp.dot(a, b.astype(BF16), preferred_element_type=F32)


def _sigmoid(x):
    return 1.0 / (1.0 + jnp.exp(-x))


def _rmsnorm_kernel(x_ref, g_ref, o_ref):
    x = x_ref[...]
    ms = jnp.mean(x * x, axis=-1, keepdims=True)
    o_ref[...] = (x * lax.rsqrt(ms + EPS) * g_ref[...]).astype(o_ref.dtype)


def _rmsnorm(x, g, out_dtype, name, block_rows=256):
    t, d = x.shape
    blocks = [((block_rows, d), F32), ((block_rows, d), out_dtype)]
    return pl.pallas_call(
        _rmsnorm_kernel,
        out_shape=jax.ShapeDtypeStruct((t, d), out_dtype),
        grid=(t // block_rows,),
        in_specs=[pl.BlockSpec((block_rows, d), lambda i: (i, 0)),
                  pl.BlockSpec((1, d), lambda i: (0, 0))],
        out_specs=pl.BlockSpec((block_rows, d), lambda i: (i, 0)),
        compiler_params=_params(("parallel",), blocks, [((block_rows, d), F32)] * 2),
        name=name,
    )(x, g.reshape(1, d))


def _rope_table(pos, dim):
    inv = ROPE_THETA ** (-jnp.arange(0, dim, 2, dtype=F32) / dim)
    ang = pos.astype(F32)[:, None] * inv[None, :]
    ang = jnp.concatenate([ang, ang], axis=-1)
    return jnp.cos(ang), jnp.sin(ang)


def _axial_tables(seq, rot_dim):
    rows = seq // GRID_W
    row_idx = jnp.repeat(jnp.arange(rows, dtype=jnp.int32), GRID_W)
    col_idx = jnp.tile(jnp.arange(GRID_W, dtype=jnp.int32), rows)
    half = rot_dim // 2
    cos_r, sin_r = _rope_table(row_idx, half)
    cos_c, sin_c = _rope_table(col_idx, half)
    cos = jnp.concatenate([cos_r, cos_c], axis=-1)
    sin = jnp.concatenate([sin_r, sin_c], axis=-1)
    first = (jnp.arange(rot_dim) % half) < (half // 2)
    sin_up = jnp.where(first[None, :], -sin, 0.0)
    sin_dn = jnp.where(first[None, :], 0.0, sin)
    pad = ((0, 0), (0, LANES - rot_dim))
    return tuple(jnp.pad(t, pad) for t in (cos, sin_up, sin_dn))


def _rope_lanes(x, cos, sin_up, sin_dn, quarter):
    up = pltpu.roll(x, LANES - quarter, 1)
    dn = pltpu.roll(x, quarter, 1)
    return x * cos + up * sin_up + dn * sin_dn


def _dot_nt(a, bt):
    return lax.dot_general(a, bt.astype(BF16), (((1,), (1,)), ((), ())), preferred_element_type=F32)


def _in_proj_latent_kernel(h_ref, wt_ref, wtk_ref, z_ref, zk_ref):
    h = h_ref[...]
    z_ref[...] = _dot_nt(h, wt_ref[...])

    @pl.when(pl.program_id(1) == 0)
    def _():
        lane = lax.broadcasted_iota(jnp.int32, (1, LANES), 1)
        zk_ref[...] = jnp.where(lane < MLA_ROPE, _dot_nt(h, wtk_ref[...]), 0.0)


def _in_proj_latent(h, w_in_t, bm, name):
    t, k = h.shape
    nb = LATENT_COLS // IN_BLOCK
    blocks = [((bm, k), BF16), ((IN_BLOCK, k), F32), ((LANES, k), F32), ((bm, IN_BLOCK), F32),
              ((bm, LANES), F32)]
    return pl.pallas_call(
        _in_proj_latent_kernel,
        out_shape=(jax.ShapeDtypeStruct((t, LATENT_COLS), F32), jax.ShapeDtypeStruct((t, LANES), F32)),
        grid=(t // bm, nb),
        in_specs=[pl.BlockSpec((bm, k), lambda i, j: (i, 0)),
                  pl.BlockSpec((IN_BLOCK, k), lambda i, j: (j, 0)),
                  pl.BlockSpec((LANES, k), lambda i, j: (LATENT_COLS // LANES, 0))],
        out_specs=(pl.BlockSpec((bm, IN_BLOCK), lambda i, j: (i, j)),
                   pl.BlockSpec((bm, LANES), lambda i, j: (i, 0))),
        compiler_params=_params(("parallel", "arbitrary"), blocks, [((bm, IN_BLOCK), F32)]),
        name=name,
    )(h, w_in_t, w_in_t)


def _in_proj_cast_kernel(h_ref, wt_ref, o_ref):
    o_ref[...] = _dot_nt(h_ref[...], wt_ref[...]).astype(o_ref.dtype)


def _in_proj_sigmoid_kernel(h_ref, wt_ref, o_ref):
    o_ref[...] = _sigmoid(_dot_nt(h_ref[...], wt_ref[...])).astype(o_ref.dtype)


def _in_proj_rows(body, h, w_in_t, first_row, n_rows, out_dtype, bm, name):
    t, k = h.shape
    resident_h = _nbytes((bm, k), BF16) >= RESIDENT_MIN_BYTES
    pipelined = [((IN_BLOCK, k), F32), ((bm, IN_BLOCK), out_dtype)] + ([] if resident_h else [((bm, k), BF16)])
    single = [((bm, IN_BLOCK), F32), ((IN_BLOCK, k), BF16)] + ([((bm, k), BF16)] if resident_h else [])
    return pl.pallas_call(
        body,
        out_shape=jax.ShapeDtypeStruct((t, n_rows), out_dtype),
        grid=(t // bm, n_rows // IN_BLOCK),
        in_specs=[pl.BlockSpec((bm, k), lambda i, j: (i, 0), pipeline_mode=_RESIDENT if resident_h else None),
                  pl.BlockSpec((pl.Element(IN_BLOCK), pl.Element(k)),
                               lambda i, j: ((first_row // SUBLANES + j * (IN_BLOCK // SUBLANES)) * SUBLANES, 0))],
        out_specs=pl.BlockSpec((bm, IN_BLOCK), lambda i, j: (i, j)),
        compiler_params=_params(("parallel", "arbitrary"), pipelined, single),
        name=name,
    )(h, w_in_t)


def _gqa_prep_kernel(z_ref, g_ref, c_ref, cos_ref, su_ref, sd_ref, o_ref, *, heads_per_step):
    cos, su, sd = cos_ref[...], su_ref[...], sd_ref[...]
    for h in range(heads_per_step):
        sl = slice(h * GQA_HEAD_DIM, (h + 1) * GQA_HEAD_DIM)
        x = z_ref[:, sl]
        ms = jnp.mean(x * x, axis=-1, keepdims=True)
        y = x * lax.rsqrt(ms + EPS) * g_ref[:, sl]
        o_ref[:, sl] = (_rope_lanes(y, cos, su, sd, GQA_HEAD_DIM // 4) * c_ref[:, sl]).astype(o_ref.dtype)


def _gqa_prep(z, gains, post_scale, tables, seq, br, heads_per_step, name):
    t, n = z.shape
    bn = heads_per_step * GQA_HEAD_DIM
    s_blocks = seq // br
    tab_spec = pl.BlockSpec((br, LANES), lambda i, j: (i % s_blocks, 0))
    row_spec = pl.BlockSpec((1, bn), lambda i, j: (0, j))
    blocks = [((br, bn), F32), ((br, bn), BF16)] + [((br, LANES), F32)] * 3
    return pl.pallas_call(
        functools.partial(_gqa_prep_kernel, heads_per_step=heads_per_step),
        out_shape=jax.ShapeDtypeStruct((t, n), BF16),
        grid=(t // br, n // bn),
        in_specs=[pl.BlockSpec((br, bn), lambda i, j: (i, j)), row_spec, row_spec,
                  tab_spec, tab_spec, tab_spec],
        out_specs=pl.BlockSpec((br, bn), lambda i, j: (i, j)),
        compiler_params=_params(("parallel", "arbitrary"), blocks, [((br, bn), F32)] * 2),
        name=name,
    )(z, gains, post_scale, *tables)


def _mla_q_kernel(z_ref, g_ref, w_ref, cos_ref, su_ref, sd_ref, o_ref, cq_ref, *, heads_per_step):
    @pl.when(pl.program_id(1) == 0)
    def _():
        z = z_ref[...]
        ms = jnp.mean(z * z, axis=-1, keepdims=True)
        cq_ref[...] = (z * lax.rsqrt(ms + EPS) * g_ref[...]).astype(cq_ref.dtype)

    q = _dot(cq_ref[...], w_ref[...]) * (MLA_SCALE * LOG2_E)
    cos, su, sd = cos_ref[...], su_ref[...], sd_ref[...]
    for h in range(heads_per_step):
        base = h * MLA_QK_PAD
        o_ref[:, base:base + MLA_NOPE] = q[:, base:base + MLA_NOPE].astype(o_ref.dtype)
        pe = q[:, base + MLA_NOPE:base + MLA_QK_PAD]
        o_ref[:, base + MLA_NOPE:base + MLA_QK_PAD] = _rope_lanes(
            pe, cos, su, sd, MLA_ROPE // 4).astype(o_ref.dtype)


def _mla_q_proj(z_lat, g_q_a, w_q, tables, seq, bm, heads_per_step, name):
    t = z_lat.shape[0]
    bn = heads_per_step * MLA_QK_PAD
    n = w_q.shape[1]
    s_blocks = seq // bm
    tab_spec = pl.BlockSpec((bm, LANES), lambda i, j: (i % s_blocks, 0))
    blocks = [((bm, Q_LORA), F32), ((Q_LORA, bn), BF16), ((bm, bn), BF16)] + [((bm, LANES), F32)] * 3
    return pl.pallas_call(
        functools.partial(_mla_q_kernel, heads_per_step=heads_per_step),
        out_shape=jax.ShapeDtypeStruct((t, n), BF16),
        grid=(t // bm, n // bn),
        in_specs=[pl.BlockSpec((bm, Q_LORA), lambda i, j: (i, 0)),
                  pl.BlockSpec((1, Q_LORA), lambda i, j: (0, 0)),
                  pl.BlockSpec((Q_LORA, bn), lambda i, j: (0, j)),
                  tab_spec, tab_spec, tab_spec],
        out_specs=pl.BlockSpec((bm, bn), lambda i, j: (i, j)),
        scratch_shapes=[pltpu.VMEM((bm, Q_LORA), BF16)],
        compiler_params=_params(("parallel", "arbitrary"), blocks,
                                [((bm, Q_LORA), BF16), ((bm, bn), F32), ((bm, Q_LORA), F32)]),
        name=name,
    )(z_lat, g_q_a.reshape(1, Q_LORA), w_q, *tables)


def _mla_kv_kernel(z_ref, pe_ref, g_ref, w_ref, cos_ref, su_ref, sd_ref, k_ref, v_ref, ckv_ref, kpe_ref,
                   *, heads_per_step):
    @pl.when(pl.program_id(1) == 0)
    def _():
        z = z_ref[...]
        ms = jnp.mean(z * z, axis=-1, keepdims=True)
        ckv_ref[...] = (z * lax.rsqrt(ms + EPS) * g_ref[...]).astype(ckv_ref.dtype)
        kpe_ref[...] = _rope_lanes(pe_ref[...], cos_ref[...], su_ref[...], sd_ref[...],
                                   MLA_ROPE // 4).astype(kpe_ref.dtype)

    kv = _dot(ckv_ref[...], w_ref[...])
    for h in range(heads_per_step):
        src = h * (MLA_NOPE + MLA_V)
        dst = h * MLA_QK_PAD
        k_ref[:, dst:dst + MLA_NOPE] = kv[:, src:src + MLA_NOPE].astype(k_ref.dtype)
        k_ref[:, dst + MLA_NOPE:dst + MLA_QK_PAD] = kpe_ref[...]
        v_ref[:, h * MLA_V:(h + 1) * MLA_V] = kv[:, src + MLA_NOPE:src + MLA_NOPE + MLA_V].astype(v_ref.dtype)


def _mla_kv_proj(z_lat, z_kpe, g_kv_a, w_kv, tables, seq, bm, heads_per_step, name):
    t = z_lat.shape[0]
    bn = heads_per_step * (MLA_NOPE + MLA_V)
    steps = w_kv.shape[1] // bn
    s_blocks = seq // bm
    tab_spec = pl.BlockSpec((bm, LANES), lambda i, j: (i % s_blocks, 0))
    kva_block = Q_LORA // KV_LORA
    bk, bv = heads_per_step * MLA_QK_PAD, heads_per_step * MLA_V
    blocks = ([((bm, KV_LORA), F32), ((bm, LANES), F32), ((KV_LORA, bn), BF16), ((bm, bk), BF16),
               ((bm, bv), BF16)] + [((bm, LANES), F32)] * 3)
    return pl.pallas_call(
        functools.partial(_mla_kv_kernel, heads_per_step=heads_per_step),
        out_shape=(jax.ShapeDtypeStruct((t, steps * bk), BF16), jax.ShapeDtypeStruct((t, steps * bv), BF16)),
        grid=(t // bm, steps),
        in_specs=[pl.BlockSpec((bm, KV_LORA), lambda i, j: (i, kva_block)),
                  pl.BlockSpec((bm, LANES), lambda i, j: (i, 0)),
                  pl.BlockSpec((1, KV_LORA), lambda i, j: (0, 0)),
                  pl.BlockSpec((KV_LORA, bn), lambda i, j: (0, j)),
                  tab_spec, tab_spec, tab_spec],
        out_specs=(pl.BlockSpec((bm, bk), lambda i, j: (i, j)),
                   pl.BlockSpec((bm, bv), lambda i, j: (i, j))),
        scratch_shapes=[pltpu.VMEM((bm, KV_LORA), BF16), pltpu.VMEM((bm, LANES), BF16)],
        compiler_params=_params(("parallel", "arbitrary"), blocks,
                                [((bm, KV_LORA), BF16), ((bm, bn), F32), ((bm, KV_LORA), F32)]),
        name=name,
    )(z_lat, z_kpe, g_kv_a.reshape(1, KV_LORA), w_kv, *tables)


def _attention_kernel(q_ref, k_ref, v_ref, o_ref, *, bq):
    k = k_ref[...]
    v = v_ref[...]
    for qi in range(q_ref.shape[0] // bq):
        rows = slice(qi * bq, (qi + 1) * bq)
        s = lax.dot_general(q_ref[rows, :], k, (((1,), (1,)), ((), ())),
                            preferred_element_type=F32)
        m = jnp.max(s, axis=-1, keepdims=True)
        p = jnp.exp2(s - m)
        l = jnp.sum(p, axis=-1, keepdims=True)
        o = jnp.dot(p.astype(BF16), v, preferred_element_type=F32)
        o_ref[rows, :] = (o / l).astype(o_ref.dtype)


def _attention(q, k, v, *, n_q_heads, group, dk, dv, k_head_offset, v_head_offset, bq, name):
    b, s, _ = q.shape
    n_blk = s // bq
    blocks = [((s, dk), BF16), ((s, dk), BF16), ((s, dv), BF16), ((s, dv), BF16)]
    return pl.pallas_call(
        functools.partial(_attention_kernel, bq=bq),
        out_shape=jax.ShapeDtypeStruct((b, s, n_q_heads * dv), BF16),
        grid=(b, n_q_heads),
        in_specs=[pl.BlockSpec((None, s, dk), lambda bi, h: (bi, 0, h)),
                  pl.BlockSpec((None, s, dk), lambda bi, h: (bi, 0, k_head_offset + h // group)),
                  pl.BlockSpec((None, s, dv), lambda bi, h: (bi, 0, v_head_offset + h // group))],
        out_specs=pl.BlockSpec((None, s, dv), lambda bi, h: (bi, 0, h)),
        compiler_params=_params(("parallel", "arbitrary"), blocks,
                                [((bq, s), F32)] * (2 * n_blk) + [((bq, s), BF16)] * n_blk),
        name=name,
    )(q, k, v)


def _merge_kernel(oa_ref, ob_ref, wa_ref, wb_ref, ga_ref, gb_ref, o_ref):
    pa = _dot(oa_ref[...], wa_ref[...])
    pb = _dot(ob_ref[...], wb_ref[...])
    o_ref[...] = (ga_ref[...].astype(F32) * pa + gb_ref[...].astype(F32) * pb).astype(o_ref.dtype)


def _gated_merge(o_a, o_b, w_a, w_b, gates, gate_block_offset, bm, bn, name):
    t, ka = o_a.shape
    kb = o_b.shape[1]
    n = w_a.shape[1]
    nb = n // bn
    pipelined = [((ka, bn), w_a.dtype), ((kb, bn), w_b.dtype), ((bm, bn), BF16), ((bm, bn), BF16),
                 ((bm, bn), BF16)]
    single = [((bm, ka), BF16), ((bm, kb), BF16)] + [((bm, bn), F32)] * 2
    return pl.pallas_call(
        _merge_kernel,
        out_shape=jax.ShapeDtypeStruct((t, n), BF16),
        grid=(t // bm, nb),
        in_specs=[pl.BlockSpec((bm, ka), lambda i, j: (i, 0), pipeline_mode=_RESIDENT),
                  pl.BlockSpec((bm, kb), lambda i, j: (i, 0), pipeline_mode=_RESIDENT),
                  pl.BlockSpec((ka, bn), lambda i, j: (0, j)),
                  pl.BlockSpec((kb, bn), lambda i, j: (0, j)),
                  pl.BlockSpec((bm, bn), lambda i, j: (i, gate_block_offset + j)),
                  pl.BlockSpec((bm, bn), lambda i, j: (i, gate_block_offset + nb + j))],
        out_specs=pl.BlockSpec((bm, bn), lambda i, j: (i, j)),
        compiler_params=_params(("parallel", "arbitrary"), pipelined, single),
        name=name,
    )(o_a, o_b, w_a, w_b, gates, gates)


def _mm_residual_kernel(a_ref, b_ref, r_ref, o_ref):
    o_ref[...] = r_ref[...] + _dot(a_ref[...], b_ref[...])


def _matmul_residual(a, b, r, bm, bn, name, k_block_index=0, k_block=None, resident_a=False):
    m = a.shape[0]
    k = b.shape[0] if k_block is None else k_block
    n = b.shape[1]
    a_block = ((bm, k), a.dtype)
    pipelined = [((k, bn), b.dtype), ((bm, bn), F32), ((bm, bn), F32)] + ([] if resident_a else [a_block])
    single = [((bm, bn), F32)] + ([a_block] if resident_a else [])
    return pl.pallas_call(
        _mm_residual_kernel,
        out_shape=jax.ShapeDtypeStruct((m, n), F32),
        grid=(m // bm, n // bn),
        in_specs=[pl.BlockSpec((bm, k), lambda i, j: (i, k_block_index),
                               pipeline_mode=_RESIDENT if resident_a else None),
                  pl.BlockSpec((k, bn), lambda i, j: (k_block_index, j)),
                  pl.BlockSpec((bm, bn), lambda i, j: (i, j))],
        out_specs=pl.BlockSpec((bm, bn), lambda i, j: (i, j)),
        compiler_params=_params(("parallel", "arbitrary"), pipelined, single),
        name=name,
    )(a, b, r)


def _swiglu_kernel(a_ref, wg_ref, wu_ref, o_ref):
    a = a_ref[...]
    g = _dot(a, wg_ref[...])
    u = _dot(a, wu_ref[...])
    o_ref[...] = (g * _sigmoid(g) * u).astype(o_ref.dtype)


def _swiglu(a, w_gate, w_up, bm, bn, name):
    t, k = a.shape
    n = w_gate.shape[1]
    pipelined = [((k, bn), w_gate.dtype), ((k, bn), w_up.dtype), ((bm, bn), BF16)]
    single = [((bm, k), BF16)] + [((bm, bn), F32)] * 3
    return pl.pallas_call(
        _swiglu_kernel,
        out_shape=jax.ShapeDtypeStruct((t, n), BF16),
        grid=(t // bm, n // bn),
        in_specs=[pl.BlockSpec((bm, k), lambda i, j: (i, 0), pipeline_mode=_RESIDENT),
                  pl.BlockSpec((k, bn), lambda i, j: (0, j)),
                  pl.BlockSpec((k, bn), lambda i, j: (0, j))],
        out_specs=pl.BlockSpec((bm, bn), lambda i, j: (i, j)),
        compiler_params=_params(("parallel", "arbitrary"), pipelined, single),
        name=name,
    )(a, w_gate, w_up)


def _layer(xf, batch, seq, g_attn, w_in, g_q_a, w_q_b, g_kv_a, w_kv_b, g_qn, g_kn,
           w_branch_a, w_branch_b, w_o, g_ffn, w_gate, w_up, w_down):
    t, d = xf.shape
    w_q = jnp.pad(w_q_b.reshape(Q_LORA, MLA_HEADS, MLA_NOPE + MLA_ROPE),
                  ((0, 0), (0, 0), (0, MLA_QK_PAD - MLA_NOPE - MLA_ROPE)))
    w_q = w_q.reshape(Q_LORA, MLA_HEADS * MLA_QK_PAD).astype(BF16)
    w_kv = w_kv_b.astype(BF16)
    gqa_gains = jnp.concatenate([jnp.tile(g_qn, GQA_Q_HEADS), jnp.tile(g_kn, GQA_KV_HEADS)]).reshape(1, -1)
    gqa_post = jnp.concatenate([jnp.full((GQA_Q_HEADS * GQA_HEAD_DIM,), GQA_SCALE * LOG2_E, F32),
                                jnp.ones((GQA_KV_HEADS * GQA_HEAD_DIM,), F32)]).reshape(1, -1)
    mla_tables = _axial_tables(seq, MLA_ROPE)
    gqa_tables = _axial_tables(seq, GQA_HEAD_DIM)

    h = _rmsnorm(xf, g_attn, BF16, "norm_attn")
    w_in_t = w_in.T
    z_lat, z_kpe = _in_proj_latent(h, w_in_t, 1024, "in_proj_latent")
    row = LATENT_COLS + MLA_ROPE
    n_qk = (GQA_Q_HEADS + GQA_KV_HEADS) * GQA_HEAD_DIM
    n_v = GQA_KV_HEADS * GQA_HEAD_DIM
    z_qk = _in_proj_rows(_in_proj_cast_kernel, h, w_in_t, row, n_qk, F32, 1024, "in_proj_qk")
    v_b = _in_proj_rows(_in_proj_cast_kernel, h, w_in_t, row + n_qk, n_v, BF16, 1024, "in_proj_v")
    gates = _in_proj_rows(_in_proj_sigmoid_kernel, h, w_in_t, row + n_qk + n_v, 2 * d, BF16, 2048,
                          "in_proj_gate")
    qk_b = _gqa_prep(z_qk, gqa_gains, gqa_post, gqa_tables, seq, 512, 4, "gqa_prep")

    q_a = _mla_q_proj(z_lat, g_q_a, w_q, mla_tables, seq, 1024, 4, "mla_q_proj")
    k_a, v_a = _mla_kv_proj(z_lat, z_kpe, g_kv_a, w_kv, mla_tables, seq, 1024, 2, "mla_kv_proj")
    o_a = _attention(q_a.reshape(batch, seq, -1), k_a.reshape(batch, seq, -1), v_a.reshape(batch, seq, -1),
                     n_q_heads=MLA_HEADS, group=1, dk=MLA_QK_PAD, dv=MLA_V, k_head_offset=0,
                     v_head_offset=0, bq=512, name="mla_attention")

    qk_b3 = qk_b.reshape(batch, seq, -1)
    o_b = _attention(qk_b3, qk_b3, v_b.reshape(batch, seq, -1), n_q_heads=GQA_Q_HEADS,
                     group=GQA_Q_HEADS // GQA_KV_HEADS, dk=GQA_HEAD_DIM, dv=GQA_HEAD_DIM,
                     k_head_offset=GQA_Q_HEADS, v_head_offset=0, bq=512, name="gqa_attention")

    m = _gated_merge(o_a.reshape(t, -1), o_b.reshape(t, -1), w_branch_a, w_branch_b, gates,
                     0, 2048, 512, "gated_merge")
    x1 = _matmul_residual(m, w_o, xf, 2048, 512, "out_proj", resident_a=True)

    h2 = _rmsnorm(x1, g_ffn, BF16, "norm_ffn")
    act = _swiglu(h2, w_gate, w_up, 2048, 256, "ffn_gate_up")
    k_half = act.shape[1] // 2
    wd = w_down.astype(BF16)
    y = _matmul_residual(act, wd, x1, 1024, 512, "ffn_down_lo", k_block_index=0, k_block=k_half)
    return _matmul_residual(act, wd, y, 1024, 512, "ffn_down_hi", k_block_index=1, k_block=k_half)


def kernel(x, g_attn, w_in, g_q_a, w_q_b, g_kv_a, w_kv_b, g_qn, g_kn, w_branch_a, w_branch_b, w_o,
           g_ffn, w_gate, w_up, w_down, g_final):
    batch, seq, d = x.shape
    xf = x.reshape(batch * seq, d)
    for l in range(g_attn.shape[0]):
        xf = _layer(xf, batch, seq, g_attn[l], w_in[l], g_q_a[l], w_q_b[l], g_kv_a[l], w_kv_b[l],
                    g_qn[l], g_kn[l], w_branch_a[l], w_branch_b[l], w_o[l], g_ffn[l],
                    w_gate[l], w_up[l], w_down[l])
    return _rmsnorm(xf, g_final, F32, "norm_final").reshape(batch, seq, d)
```

```python
import functools

import jax
import jax.numpy as jnp
import numpy as np
from jax import lax
from jax.experimental import pallas as pl
from jax.experimental.pallas import tpu as pltpu

GRID_W = 64
ROPE_THETA = 10000.0
EPS = 1e-6

MLA_HEADS = 16
MLA_NOPE = 128
MLA_ROPE = 64
MLA_V = 128
MLA_QK_PAD = 256
Q_LORA = 1024
KV_LORA = 512
MLA_SCALE = (MLA_NOPE + MLA_ROPE) ** -0.5

GQA_Q_HEADS = 16
GQA_KV_HEADS = 4
GQA_HEAD_DIM = 128
GQA_SCALE = GQA_HEAD_DIM ** -0.5
LOG2_E = 1.4426950408889634

LANES = 128
SUBLANES = 8
V7X_VMEM_BYTES = 64 * 1024 * 1024
VMEM_CAP_BYTES = V7X_VMEM_BYTES - 6 * 1024 * 1024

IN_BLOCK = 512
LATENT_COLS = Q_LORA + KV_LORA

BF16 = jnp.bfloat16
F32 = jnp.float32

_RESIDENT = pl.Buffered(1)
RESIDENT_MIN_BYTES = 16 * 1024 * 1024


def _nbytes(shape, dtype):
    return int(np.prod(shape)) * jnp.dtype(dtype).itemsize


def _params(semantics, pipelined_blocks, single_blocks=()):
    est = 2 * sum(_nbytes(s, d) for s, d in pipelined_blocks)
    est += sum(_nbytes(s, d) for s, d in single_blocks)
    limit = min(VMEM_CAP_BYTES, int(est * 1.25) + (4 << 20))
    return pltpu.CompilerParams(dimension_semantics=semantics, vmem_limit_bytes=limit)


def _dot(a, b):
    return jnp.dot(a, b.astype(BF16), preferred_element_type=F32)


def _sigmoid(x):
    return 0.5 * jnp.tanh(0.5 * x) + 0.5


def _rmsnorm_kernel(x_ref, g_ref, o_ref):
    x = x_ref[...]
    ms = jnp.mean(x * x, axis=-1, keepdims=True)
    o_ref[...] = (x * lax.rsqrt(ms + EPS) * g_ref[...]).astype(o_ref.dtype)


def _rmsnorm(x, g, out_dtype, name, block_rows=256):
    t, d = x.shape
    blocks = [((block_rows, d), F32), ((block_rows, d), out_dtype)]
    return pl.pallas_call(
        _rmsnorm_kernel,
        out_shape=jax.ShapeDtypeStruct((t, d), out_dtype),
        grid=(t // block_rows,),
        in_specs=[pl.BlockSpec((block_rows, d), lambda i: (i, 0)),
                  pl.BlockSpec((1, d), lambda i: (0, 0))],
        out_specs=pl.BlockSpec((block_rows, d), lambda i: (i, 0)),
        compiler_params=_params(("parallel",), blocks, [((block_rows, d), F32)] * 2),
        name=name,
    )(x, g.reshape(1, d))


def _rope_table(pos, dim):
    inv = ROPE_THETA ** (-jnp.arange(0, dim, 2, dtype=F32) / dim)
    ang = pos.astype(F32)[:, None] * inv[None, :]
    ang = jnp.concatenate([ang, ang], axis=-1)
    return jnp.cos(ang), jnp.sin(ang)


def _axial_tables(seq, rot_dim):
    rows = seq // GRID_W
    row_idx = jnp.repeat(jnp.arange(rows, dtype=jnp.int32), GRID_W)
    col_idx = jnp.tile(jnp.arange(GRID_W, dtype=jnp.int32), rows)
    half = rot_dim // 2
    cos_r, sin_r = _rope_table(row_idx, half)
    cos_c, sin_c = _rope_table(col_idx, half)
    cos = jnp.concatenate([cos_r, cos_c], axis=-1)
    sin = jnp.concatenate([sin_r, sin_c], axis=-1)
    first = (jnp.arange(rot_dim) % half) < (half // 2)
    sin_up = jnp.where(first[None, :], -sin, 0.0)
    sin_dn = jnp.where(first[None, :], 0.0, sin)
    pad = ((0, 0), (0, LANES - rot_dim))
    return tuple(jnp.pad(t, pad) for t in (cos, sin_up, sin_dn))


def _rope_lanes(x, cos, sin_up, sin_dn, quarter):
    up = pltpu.roll(x, LANES - quarter, 1)
    dn = pltpu.roll(x, quarter, 1)
    return x * cos + up * sin_up + dn * sin_dn


def _dot_nt(a, bt):
    return lax.dot_general(a, bt.astype(BF16), (((1,), (1,)), ((), ())), preferred_element_type=F32)


def _in_proj_latent_kernel(h_ref, wt_ref, wtk_ref, z_ref, zk_ref):
    h = h_ref[...]
    z_ref[...] = _dot_nt(h, wt_ref[...])

    @pl.when(pl.program_id(1) == 0)
    def _():
        lane = lax.broadcasted_iota(jnp.int32, (1, LANES), 1)
        zk_ref[...] = jnp.where(lane < MLA_ROPE, _dot_nt(h, wtk_ref[...]), 0.0)


def _in_proj_latent(h, w_in_t, bm, name):
    t, k = h.shape
    nb = LATENT_COLS // IN_BLOCK
    blocks = [((bm, k), BF16), ((IN_BLOCK, k), F32), ((LANES, k), F32), ((bm, IN_BLOCK), F32),
              ((bm, LANES), F32)]
    return pl.pallas_call(
        _in_proj_latent_kernel,
        out_shape=(jax.ShapeDtypeStruct((t, LATENT_COLS), F32), jax.ShapeDtypeStruct((t, LANES), F32)),
        grid=(t // bm, nb),
        in_specs=[pl.BlockSpec((bm, k), lambda i, j: (i, 0)),
                  pl.BlockSpec((IN_BLOCK, k), lambda i, j: (j, 0)),
                  pl.BlockSpec((LANES, k), lambda i, j: (LATENT_COLS // LANES, 0))],
        out_specs=(pl.BlockSpec((bm, IN_BLOCK), lambda i, j: (i, j)),
                   pl.BlockSpec((bm, LANES), lambda i, j: (i, 0))),
        compiler_params=_params(("parallel", "arbitrary"), blocks, [((bm, IN_BLOCK), F32)]),
        name=name,
    )(h, w_in_t, w_in_t)


def _in_proj_cast_kernel(h_ref, wt_ref, o_ref):
    o_ref[...] = _dot_nt(h_ref[...], wt_ref[...]).astype(o_ref.dtype)


def _in_proj_sigmoid_kernel(h_ref, wt_ref, o_ref):
    o_ref[...] = _sigmoid(_dot_nt(h_ref[...], wt_ref[...])).astype(o_ref.dtype)


def _in_proj_rows(body, h, w_in_t, first_row, n_rows, out_dtype, bm, name):
    t, k = h.shape
    resident_h = _nbytes((bm, k), BF16) >= RESIDENT_MIN_BYTES
    pipelined = [((IN_BLOCK, k), F32), ((bm, IN_BLOCK), out_dtype)] + ([] if resident_h else [((bm, k), BF16)])
    single = [((bm, IN_BLOCK), F32), ((IN_BLOCK, k), BF16)] + ([((bm, k), BF16)] if resident_h else [])
    return pl.pallas_call(
        body,
        out_shape=jax.ShapeDtypeStruct((t, n_rows), out_dtype),
        grid=(t // bm, n_rows // IN_BLOCK),
        in_specs=[pl.BlockSpec((bm, k), lambda i, j: (i, 0), pipeline_mode=_RESIDENT if resident_h else None),
                  pl.BlockSpec((pl.Element(IN_BLOCK), pl.Element(k)),
                               lambda i, j: ((first_row // SUBLANES + j * (IN_BLOCK // SUBLANES)) * SUBLANES, 0))],
        out_specs=pl.BlockSpec((bm, IN_BLOCK), lambda i, j: (i, j)),
        compiler_params=_params(("parallel", "arbitrary"), pipelined, single),
        name=name,
    )(h, w_in_t)


def _gqa_prep_kernel(z_ref, g_ref, c_ref, cos_ref, su_ref, sd_ref, o_ref, *, heads_per_step):
    cos, su, sd = cos_ref[...], su_ref[...], sd_ref[...]
    for h in range(heads_per_step):
        sl = slice(h * GQA_HEAD_DIM, (h + 1) * GQA_HEAD_DIM)
        x = z_ref[:, sl]
        ms = jnp.mean(x * x, axis=-1, keepdims=True)
        y = x * lax.rsqrt(ms + EPS) * g_ref[:, sl]
        o_ref[:, sl] = (_rope_lanes(y, cos, su, sd, GQA_HEAD_DIM // 4) * c_ref[:, sl]).astype(o_ref.dtype)


def _gqa_prep(z, gains, post_scale, tables, seq, br, heads_per_step, name):
    t, n = z.shape
    bn = heads_per_step * GQA_HEAD_DIM
    s_blocks = seq // br
    tab_spec = pl.BlockSpec((br, LANES), lambda i, j: (i % s_blocks, 0))
    row_spec = pl.BlockSpec((1, bn), lambda i, j: (0, j))
    blocks = [((br, bn), F32), ((br, bn), BF16)] + [((br, LANES), F32)] * 3
    return pl.pallas_call(
        functools.partial(_gqa_prep_kernel, heads_per_step=heads_per_step),
        out_shape=jax.ShapeDtypeStruct((t, n), BF16),
        grid=(t // br, n // bn),
        in_specs=[pl.BlockSpec((br, bn), lambda i, j: (i, j)), row_spec, row_spec,
                  tab_spec, tab_spec, tab_spec],
        out_specs=pl.BlockSpec((br, bn), lambda i, j: (i, j)),
        compiler_params=_params(("parallel", "arbitrary"), blocks, [((br, bn), F32)] * 2),
        name=name,
    )(z, gains, post_scale, *tables)


def _mla_q_kernel(z_ref, g_ref, w_ref, cos_ref, su_ref, sd_ref, o_ref, cq_ref, *, heads_per_step):
    @pl.when(pl.program_id(1) == 0)
    def _():
        z = z_ref[...]
        ms = jnp.mean(z * z, axis=-1, keepdims=True)
        cq_ref[...] = (z * lax.rsqrt(ms + EPS) * g_ref[...]).astype(cq_ref.dtype)

    q = _dot(cq_ref[...], w_ref[...]) * (MLA_SCALE * LOG2_E)
    cos, su, sd = cos_ref[...], su_ref[...], sd_ref[...]
    for h in range(heads_per_step):
        base = h * MLA_QK_PAD
        o_ref[:, base:base + MLA_NOPE] = q[:, base:base + MLA_NOPE].astype(o_ref.dtype)
        pe = q[:, base + MLA_NOPE:base + MLA_QK_PAD]
        o_ref[:, base + MLA_NOPE:base + MLA_QK_PAD] = _rope_lanes(
            pe, cos, su, sd, MLA_ROPE // 4).astype(o_ref.dtype)


def _mla_q_proj(z_lat, g_q_a, w_q, tables, seq, bm, heads_per_step, name):
    t = z_lat.shape[0]
    bn = heads_per_step * MLA_QK_PAD
    n = w_q.shape[1]
    s_blocks = seq // bm
    tab_spec = pl.BlockSpec((bm, LANES), lambda i, j: (i % s_blocks, 0))
    blocks = [((bm, Q_LORA), F32), ((Q_LORA, bn), BF16), ((bm, bn), BF16)] + [((bm, LANES), F32)] * 3
    return pl.pallas_call(
        functools.partial(_mla_q_kernel, heads_per_step=heads_per_step),
        out_shape=jax.ShapeDtypeStruct((t, n), BF16),
        grid=(t // bm, n // bn),
        in_specs=[pl.BlockSpec((bm, Q_LORA), lambda i, j: (i, 0)),
                  pl.BlockSpec((1, Q_LORA), lambda i, j: (0, 0)),
                  pl.BlockSpec((Q_LORA, bn), lambda i, j: (0, j)),
                  tab_spec, tab_spec, tab_spec],
        out_specs=pl.BlockSpec((bm, bn), lambda i, j: (i, j)),
        scratch_shapes=[pltpu.VMEM((bm, Q_LORA), BF16)],
        compiler_params=_params(("parallel", "arbitrary"), blocks,
                                [((bm, Q_LORA), BF16), ((bm, bn), F32), ((bm, Q_LORA), F32)]),
        name=name,
    )(z_lat, g_q_a.reshape(1, Q_LORA), w_q, *tables)


def _mla_kv_kernel(z_ref, pe_ref, g_ref, w_ref, cos_ref, su_ref, sd_ref, k_ref, v_ref, ckv_ref, kpe_ref,
                   *, heads_per_step):
    @pl.when(pl.program_id(1) == 0)
    def _():
        z = z_ref[...]
        ms = jnp.mean(z * z, axis=-1, keepdims=True)
        ckv_ref[...] = (z * lax.rsqrt(ms + EPS) * g_ref[...]).astype(ckv_ref.dtype)
        kpe_ref[...] = _rope_lanes(pe_ref[...], cos_ref[...], su_ref[...], sd_ref[...],
                                   MLA_ROPE // 4).astype(kpe_ref.dtype)

    kv = _dot(ckv_ref[...], w_ref[...])
    for h in range(heads_per_step):
        src = h * (MLA_NOPE + MLA_V)
        dst = h * MLA_QK_PAD
        k_ref[:, dst:dst + MLA_NOPE] = kv[:, src:src + MLA_NOPE].astype(k_ref.dtype)
        k_ref[:, dst + MLA_NOPE:dst + MLA_QK_PAD] = kpe_ref[...]
        v_ref[:, h * MLA_V:(h + 1) * MLA_V] = kv[:, src + MLA_NOPE:src + MLA_NOPE + MLA_V].astype(v_ref.dtype)


def _mla_kv_proj(z_lat, z_kpe, g_kv_a, w_kv, tables, seq, bm, heads_per_step, name):
    t = z_lat.shape[0]
    bn = heads_per_step * (MLA_NOPE + MLA_V)
    steps = w_kv.shape[1] // bn
    s_blocks = seq // bm
    tab_spec = pl.BlockSpec((bm, LANES), lambda i, j: (i % s_blocks, 0))
    kva_block = Q_LORA // KV_LORA
    bk, bv = heads_per_step * MLA_QK_PAD, heads_per_step * MLA_V
    blocks = ([((bm, KV_LORA), F32), ((bm, LANES), F32), ((KV_LORA, bn), BF16), ((bm, bk), BF16),
               ((bm, bv), BF16)] + [((bm, LANES), F32)] * 3)
    return pl.pallas_call(
        functools.partial(_mla_kv_kernel, heads_per_step=heads_per_step),
        out_shape=(jax.ShapeDtypeStruct((t, steps * bk), BF16), jax.ShapeDtypeStruct((t, steps * bv), BF16)),
        grid=(t // bm, steps),
        in_specs=[pl.BlockSpec((bm, KV_LORA), lambda i, j: (i, kva_block)),
                  pl.BlockSpec((bm, LANES), lambda i, j: (i, 0)),
                  pl.BlockSpec((1, KV_LORA), lambda i, j: (0, 0)),
                  pl.BlockSpec((KV_LORA, bn), lambda i, j: (0, j)),
                  tab_spec, tab_spec, tab_spec],
        out_specs=(pl.BlockSpec((bm, bk), lambda i, j: (i, j)),
                   pl.BlockSpec((bm, bv), lambda i, j: (i, j))),
        scratch_shapes=[pltpu.VMEM((bm, KV_LORA), BF16), pltpu.VMEM((bm, LANES), BF16)],
        compiler_params=_params(("parallel", "arbitrary"), blocks,
                                [((bm, KV_LORA), BF16), ((bm, bn), F32), ((bm, KV_LORA), F32)]),
        name=name,
    )(z_lat, z_kpe, g_kv_a.reshape(1, KV_LORA), w_kv, *tables)


def _attention_kernel(q_ref, k_ref, v_ref, o_ref, *, bq, q_heads, group, dk, dv):
    for h in range(q_heads):
        g = h // group
        k = k_ref[:, g * dk:(g + 1) * dk]
        v = v_ref[:, g * dv:(g + 1) * dv]
        for qi in range(q_ref.shape[0] // bq):
            rows = slice(qi * bq, (qi + 1) * bq)
            s = lax.dot_general(q_ref[rows, h * dk:(h + 1) * dk], k, (((1,), (1,)), ((), ())),
                                preferred_element_type=F32)
            m = jnp.max(s, axis=-1, keepdims=True)
            p = jnp.exp2(s - m)
            l = jnp.sum(p, axis=-1, keepdims=True)
            o = jnp.dot(p.astype(BF16), v, preferred_element_type=F32)
            o_ref[rows, h * dv:(h + 1) * dv] = (o / l).astype(o_ref.dtype)


def _attention(q, k, v, *, n_q_heads, group, q_heads_per_step, dk, dv, k_head_offset, v_head_offset, bq,
               name):
    b, s, _ = q.shape
    nq = q_heads_per_step
    nk = max(1, nq // group)
    assert (nq % group == 0 or group % nq == 0) and k_head_offset % nk == 0 and v_head_offset % nk == 0
    chains = nq * (s // bq)
    blocks = [((s, nq * dk), BF16), ((s, nk * dk), BF16), ((s, nk * dv), BF16), ((s, nq * dv), BF16)]
    return pl.pallas_call(
        functools.partial(_attention_kernel, bq=bq, q_heads=nq, group=group, dk=dk, dv=dv),
        out_shape=jax.ShapeDtypeStruct((b, s, n_q_heads * dv), BF16),
        grid=(b, n_q_heads // nq),
        in_specs=[pl.BlockSpec((None, s, nq * dk), lambda bi, h: (bi, 0, h)),
                  pl.BlockSpec((None, s, nk * dk), lambda bi, h: (bi, 0, (k_head_offset + h * nq // group) // nk)),
                  pl.BlockSpec((None, s, nk * dv), lambda bi, h: (bi, 0, (v_head_offset + h * nq // group) // nk))],
        out_specs=pl.BlockSpec((None, s, nq * dv), lambda bi, h: (bi, 0, h)),
        compiler_params=_params(("parallel", "arbitrary"), blocks,
                                [((bq, s), F32)] * chains + [((bq, s), BF16)] * chains),
        name=name,
    )(q, k, v)


def _merge_kernel(oa_ref, ob_ref, wa_ref, wb_ref, ga_ref, gb_ref, o_ref):
    pa = _dot(oa_ref[...], wa_ref[...])
    pb = _dot(ob_ref[...], wb_ref[...])
    o_ref[...] = (ga_ref[...].astype(F32) * pa + gb_ref[...].astype(F32) * pb).astype(o_ref.dtype)


def _gated_merge(o_a, o_b, w_a, w_b, gates, gate_block_offset, bm, bn, name):
    t, ka = o_a.shape
    kb = o_b.shape[1]
    n = w_a.shape[1]
    nb = n // bn
    pipelined = [((ka, bn), w_a.dtype), ((kb, bn), w_b.dtype), ((bm, bn), BF16), ((bm, bn), BF16),
                 ((bm, bn), BF16)]
    single = [((bm, ka), BF16), ((bm, kb), BF16)] + [((bm, bn), F32)] * 2
    return pl.pallas_call(
        _merge_kernel,
        out_shape=jax.ShapeDtypeStruct((t, n), BF16),
        grid=(t // bm, nb),
        in_specs=[pl.BlockSpec((bm, ka), lambda i, j: (i, 0), pipeline_mode=_RESIDENT),
                  pl.BlockSpec((bm, kb), lambda i, j: (i, 0), pipeline_mode=_RESIDENT),
                  pl.BlockSpec((ka, bn), lambda i, j: (0, j)),
                  pl.BlockSpec((kb, bn), lambda i, j: (0, j)),
                  pl.BlockSpec((bm, bn), lambda i, j: (i, gate_block_offset + j)),
                  pl.BlockSpec((bm, bn), lambda i, j: (i, gate_block_offset + nb + j))],
        out_specs=pl.BlockSpec((bm, bn), lambda i, j: (i, j)),
        compiler_params=_params(("parallel", "arbitrary"), pipelined, single),
        name=name,
    )(o_a, o_b, w_a, w_b, gates, gates)


def _mm_residual_kernel(a_ref, b_ref, r_ref, o_ref):
    o_ref[...] = r_ref[...] + _dot(a_ref[...], b_ref[...])


def _matmul_residual(a, b, r, bm, bn, name, k_block_index=0, k_block=None, resident_a=False):
    m = a.shape[0]
    k = b.shape[0] if k_block is None else k_block
    n = b.shape[1]
    a_block = ((bm, k), a.dtype)
    pipelined = [((k, bn), b.dtype), ((bm, bn), F32), ((bm, bn), F32)] + ([] if resident_a else [a_block])
    single = [((bm, bn), F32)] + ([a_block] if resident_a else [])
    return pl.pallas_call(
        _mm_residual_kernel,
        out_shape=jax.ShapeDtypeStruct((m, n), F32),
        grid=(m // bm, n // bn),
        in_specs=[pl.BlockSpec((bm, k), lambda i, j: (i, k_block_index),
                               pipeline_mode=_RESIDENT if resident_a else None),
                  pl.BlockSpec((k, bn), lambda i, j: (k_block_index, j)),
                  pl.BlockSpec((bm, bn), lambda i, j: (i, j))],
        out_specs=pl.BlockSpec((bm, bn), lambda i, j: (i, j)),
        compiler_params=_params(("parallel", "arbitrary"), pipelined, single),
        name=name,
    )(a, b, r)


def _swiglu_kernel(a_ref, wg_ref, wu_ref, wd_ref, o_ref, wd16_ref):
    a = a_ref[...]
    g = _dot(a, wg_ref[...])
    u = _dot(a, wu_ref[...])
    o_ref[...] = (g * _sigmoid(g) * u).astype(o_ref.dtype)

    @pl.when(pl.program_id(0) == 0)
    def _():
        wd16_ref[...] = wd_ref[...].astype(wd16_ref.dtype)


def _swiglu(a, w_gate, w_up, w_down, bm, bn, name):
    t, k = a.shape
    n = w_gate.shape[1]
    nj = n // bn
    d_out = w_down.shape[1]
    chunk = w_down.shape[0] // nj

    def wd_index(i, j):
        return (jnp.where(i == 0, j, nj - 1), 0)

    pipelined = [((k, bn), w_gate.dtype), ((k, bn), w_up.dtype), ((bm, bn), BF16),
                 ((chunk, d_out), w_down.dtype), ((chunk, d_out), BF16)]
    single = [((bm, k), BF16)] + [((bm, bn), F32)] * 3
    return pl.pallas_call(
        _swiglu_kernel,
        out_shape=(jax.ShapeDtypeStruct((t, n), BF16), jax.ShapeDtypeStruct(w_down.shape, BF16)),
        grid=(t // bm, nj),
        in_specs=[pl.BlockSpec((bm, k), lambda i, j: (i, 0), pipeline_mode=_RESIDENT),
                  pl.BlockSpec((k, bn), lambda i, j: (0, j)),
                  pl.BlockSpec((k, bn), lambda i, j: (0, j)),
                  pl.BlockSpec((chunk, d_out), wd_index)],
        out_specs=(pl.BlockSpec((bm, bn), lambda i, j: (i, j)),
                   pl.BlockSpec((chunk, d_out), wd_index)),
        compiler_params=_params(("arbitrary", "arbitrary"), pipelined, single),
        name=name,
    )(a, w_gate, w_up, w_down)


def _layer(xf, batch, seq, g_attn, w_in, g_q_a, w_q_b, g_kv_a, w_kv_b, g_qn, g_kn,
           w_branch_a, w_branch_b, w_o, g_ffn, w_gate, w_up, w_down):
    t, d = xf.shape
    w_q = jnp.pad(w_q_b.reshape(Q_LORA, MLA_HEADS, MLA_NOPE + MLA_ROPE),
                  ((0, 0), (0, 0), (0, MLA_QK_PAD - MLA_NOPE - MLA_ROPE)))
    w_q = w_q.reshape(Q_LORA, MLA_HEADS * MLA_QK_PAD).astype(BF16)
    w_kv = w_kv_b.astype(BF16)
    gqa_gains = jnp.concatenate([jnp.tile(g_qn, GQA_Q_HEADS), jnp.tile(g_kn, GQA_KV_HEADS)]).reshape(1, -1)
    gqa_post = jnp.concatenate([jnp.full((GQA_Q_HEADS * GQA_HEAD_DIM,), GQA_SCALE * LOG2_E, F32),
                                jnp.ones((GQA_KV_HEADS * GQA_HEAD_DIM,), F32)]).reshape(1, -1)
    mla_tables = _axial_tables(seq, MLA_ROPE)
    gqa_tables = _axial_tables(seq, GQA_HEAD_DIM)

    h = _rmsnorm(xf, g_attn, BF16, "norm_attn")
    w_in_t = w_in.T
    z_lat, z_kpe = _in_proj_latent(h, w_in_t, 1024, "in_proj_latent")
    row = LATENT_COLS + MLA_ROPE
    n_qk = (GQA_Q_HEADS + GQA_KV_HEADS) * GQA_HEAD_DIM
    n_v = GQA_KV_HEADS * GQA_HEAD_DIM
    z_qk = _in_proj_rows(_in_proj_cast_kernel, h, w_in_t, row, n_qk, F32, 1024, "in_proj_qk")
    v_b = _in_proj_rows(_in_proj_cast_kernel, h, w_in_t, row + n_qk, n_v, BF16, 1024, "in_proj_v")
    gates = _in_proj_rows(_in_proj_sigmoid_kernel, h, w_in_t, row + n_qk + n_v, 2 * d, BF16, 2048,
                          "in_proj_gate")
    qk_b = _gqa_prep(z_qk, gqa_gains, gqa_post, gqa_tables, seq, 512, 4, "gqa_prep")

    q_a = _mla_q_proj(z_lat, g_q_a, w_q, mla_tables, seq, 1024, 4, "mla_q_proj")
    k_a, v_a = _mla_kv_proj(z_lat, z_kpe, g_kv_a, w_kv, mla_tables, seq, 1024, 8, "mla_kv_proj")
    o_a = _attention(q_a.reshape(batch, seq, -1), k_a.reshape(batch, seq, -1), v_a.reshape(batch, seq, -1),
                     n_q_heads=MLA_HEADS, group=1, q_heads_per_step=2, dk=MLA_QK_PAD, dv=MLA_V,
                     k_head_offset=0, v_head_offset=0, bq=512, name="mla_attention")

    qk_b3 = qk_b.reshape(batch, seq, -1)
    o_b = _attention(qk_b3, qk_b3, v_b.reshape(batch, seq, -1), n_q_heads=GQA_Q_HEADS,
                     group=GQA_Q_HEADS // GQA_KV_HEADS, q_heads_per_step=2, dk=GQA_HEAD_DIM,
                     dv=GQA_HEAD_DIM, k_head_offset=GQA_Q_HEADS, v_head_offset=0, bq=512,
                     name="gqa_attention")

    m = _gated_merge(o_a.reshape(t, -1), o_b.reshape(t, -1), w_branch_a, w_branch_b, gates,
                     0, 2048, 512, "gated_merge")
    x1 = _matmul_residual(m, w_o, xf, 2048, 512, "out_proj", resident_a=True)

    h2 = _rmsnorm(x1, g_ffn, BF16, "norm_ffn")
    act, wd = _swiglu(h2, w_gate, w_up, w_down, 2048, 256, "ffn_gate_up")
    k_half = act.shape[1] // 2
    y = _matmul_residual(act, wd, x1, 1024, 512, "ffn_down_lo", k_block_index=0, k_block=k_half)
    return _matmul_residual(act, wd, y, 1024, 512, "ffn_down_hi", k_block_index=1, k_block=k_half)


def kernel(x, g_attn, w_in, g_q_a, w_q_b, g_kv_a, w_kv_b, g_qn, g_kn, w_branch_a, w_branch_b, w_o,
           g_ffn, w_gate, w_up, w_down, g_final):
    batch, seq, d = x.shape
    xf = x.reshape(batch * seq, d)
    for l in range(g_attn.shape[0]):
        xf = _layer(xf, batch, seq, g_attn[l], w_in[l], g_q_a[l], w_q_b[l], g_kv_a[l], w_kv_b[l],
                    g_qn[l], g_kn[l], w_branch_a[l], w_branch_b[l], w_o[l], g_ffn[l],
                    w_gate[l], w_up[l], w_down[l])
    return _rmsnorm(xf, g_final, F32, "norm_final").reshape(batch, seq, d)
```

```python
import functools

import jax
import jax.numpy as jnp
import numpy as np
from jax import lax
from jax.experimental import pallas as pl
from jax.experimental.pallas import tpu as pltpu

GRID_W = 64
ROPE_THETA = 10000.0
EPS = 1e-6

MLA_HEADS = 16
MLA_NOPE = 128
MLA_ROPE = 64
MLA_V = 128
MLA_QK_PAD = 256
Q_LORA = 1024
KV_LORA = 512
MLA_SCALE = (MLA_NOPE + MLA_ROPE) ** -0.5

GQA_Q_HEADS = 16
GQA_KV_HEADS = 4
GQA_HEAD_DIM = 128
GQA_SCALE = GQA_HEAD_DIM ** -0.5
LOG2_E = 1.4426950408889634

LANES = 128
SUBLANES = 8
V7X_VMEM_BYTES = 64 * 1024 * 1024
VMEM_CAP_BYTES = V7X_VMEM_BYTES - 6 * 1024 * 1024

IN_BLOCK = 512
ROW_CHAIN = 256
LATENT_COLS = Q_LORA + KV_LORA

BF16 = jnp.bfloat16
F32 = jnp.float32

_RESIDENT = pl.Buffered(1)
RESIDENT_MIN_BYTES = 16 * 1024 * 1024


def _nbytes(shape, dtype):
    return int(np.prod(shape)) * jnp.dtype(dtype).itemsize


def _params(semantics, pipelined_blocks, single_blocks=()):
    est = 2 * sum(_nbytes(s, d) for s, d in pipelined_blocks)
    est += sum(_nbytes(s, d) for s, d in single_blocks)
    limit = min(VMEM_CAP_BYTES, int(est * 1.25) + (4 << 20))
    return pltpu.CompilerParams(dimension_semantics=semantics, vmem_limit_bytes=limit)


def _dot(a, b):
    return jnp.dot(a, b.astype(BF16), preferred_element_type=F32)


def _sigmoid(x):
    return 0.5 * jnp.tanh(0.5 * x) + 0.5


def _rmsnorm_kernel(x_ref, g_ref, o_ref):
    x = x_ref[...]
    ms = jnp.mean(x * x, axis=-1, keepdims=True)
    o_ref[...] = (x * lax.rsqrt(ms + EPS) * g_ref[...]).astype(o_ref.dtype)


def _rmsnorm(x, g, out_dtype, name, block_rows=256):
    t, d = x.shape
    blocks = [((block_rows, d), F32), ((block_rows, d), out_dtype)]
    return pl.pallas_call(
        _rmsnorm_kernel,
        out_shape=jax.ShapeDtypeStruct((t, d), out_dtype),
        grid=(t // block_rows,),
        in_specs=[pl.BlockSpec((block_rows, d), lambda i: (i, 0)),
                  pl.BlockSpec((1, d), lambda i: (0, 0))],
        out_specs=pl.BlockSpec((block_rows, d), lambda i: (i, 0)),
        compiler_params=_params(("parallel",), blocks, [((block_rows, d), F32)] * 2),
        name=name,
    )(x, g.reshape(1, d))


def _rope_table(pos, dim):
    inv = ROPE_THETA ** (-jnp.arange(0, dim, 2, dtype=F32) / dim)
    ang = pos.astype(F32)[:, None] * inv[None, :]
    ang = jnp.concatenate([ang, ang], axis=-1)
    return jnp.cos(ang), jnp.sin(ang)


def _axial_tables(seq, rot_dim):
    rows = seq // GRID_W
    row_idx = jnp.repeat(jnp.arange(rows, dtype=jnp.int32), GRID_W)
    col_idx = jnp.tile(jnp.arange(GRID_W, dtype=jnp.int32), rows)
    half = rot_dim // 2
    cos_r, sin_r = _rope_table(row_idx, half)
    cos_c, sin_c = _rope_table(col_idx, half)
    cos = jnp.concatenate([cos_r, cos_c], axis=-1)
    sin = jnp.concatenate([sin_r, sin_c], axis=-1)
    first = (jnp.arange(rot_dim) % half) < (half // 2)
    sin_up = jnp.where(first[None, :], -sin, 0.0)
    sin_dn = jnp.where(first[None, :], 0.0, sin)
    pad = ((0, 0), (0, LANES - rot_dim))
    return tuple(jnp.pad(t, pad) for t in (cos, sin_up, sin_dn))


def _rope_lanes(x, cos, sin_up, sin_dn, quarter):
    up = pltpu.roll(x, LANES - quarter, 1)
    dn = pltpu.roll(x, quarter, 1)
    return x * cos + up * sin_up + dn * sin_dn


def _dot_nt(a, bt):
    return lax.dot_general(a, bt.astype(BF16), (((1,), (1,)), ((), ())), preferred_element_type=F32)


def _in_proj_latent_kernel(h_ref, wt_ref, wtk_ref, z_ref, zk_ref):
    h = h_ref[...]
    z_ref[...] = _dot_nt(h, wt_ref[...])

    @pl.when(pl.program_id(1) == 0)
    def _():
        lane = lax.broadcasted_iota(jnp.int32, (1, LANES), 1)
        zk_ref[...] = jnp.where(lane < MLA_ROPE, _dot_nt(h, wtk_ref[...]), 0.0)


def _in_proj_latent(h, w_in_t, bm, name):
    t, k = h.shape
    nb = LATENT_COLS // IN_BLOCK
    blocks = [((bm, k), BF16), ((IN_BLOCK, k), F32), ((LANES, k), F32), ((bm, IN_BLOCK), F32),
              ((bm, LANES), F32)]
    return pl.pallas_call(
        _in_proj_latent_kernel,
        out_shape=(jax.ShapeDtypeStruct((t, LATENT_COLS), F32), jax.ShapeDtypeStruct((t, LANES), F32)),
        grid=(t // bm, nb),
        in_specs=[pl.BlockSpec((bm, k), lambda i, j: (i, 0)),
                  pl.BlockSpec((IN_BLOCK, k), lambda i, j: (j, 0)),
                  pl.BlockSpec((LANES, k), lambda i, j: (LATENT_COLS // LANES, 0))],
        out_specs=(pl.BlockSpec((bm, IN_BLOCK), lambda i, j: (i, j)),
                   pl.BlockSpec((bm, LANES), lambda i, j: (i, 0))),
        compiler_params=_params(("parallel", "arbitrary"), blocks, [((bm, IN_BLOCK), F32)]),
        name=name,
    )(h, w_in_t, w_in_t)


def _in_proj_cast_kernel(h_ref, wt_ref, o_ref):
    o_ref[...] = _dot_nt(h_ref[...], wt_ref[...]).astype(o_ref.dtype)


def _in_proj_sigmoid_kernel(h_ref, wt_ref, o_ref):
    o_ref[...] = _sigmoid(_dot_nt(h_ref[...], wt_ref[...])).astype(o_ref.dtype)


def _in_proj_rows(body, h, w_in_t, first_row, n_rows, out_dtype, bm, name):
    t, k = h.shape
    resident_h = _nbytes((bm, k), BF16) >= RESIDENT_MIN_BYTES
    pipelined = [((IN_BLOCK, k), F32), ((bm, IN_BLOCK), out_dtype)] + ([] if resident_h else [((bm, k), BF16)])
    single = [((bm, IN_BLOCK), F32), ((IN_BLOCK, k), BF16)] + ([((bm, k), BF16)] if resident_h else [])
    return pl.pallas_call(
        body,
        out_shape=jax.ShapeDtypeStruct((t, n_rows), out_dtype),
        grid=(t // bm, n_rows // IN_BLOCK),
        in_specs=[pl.BlockSpec((bm, k), lambda i, j: (i, 0), pipeline_mode=_RESIDENT if resident_h else None),
                  pl.BlockSpec((pl.Element(IN_BLOCK), pl.Element(k)),
                               lambda i, j: ((first_row // SUBLANES + j * (IN_BLOCK // SUBLANES)) * SUBLANES, 0))],
        out_specs=pl.BlockSpec((bm, IN_BLOCK), lambda i, j: (i, j)),
        compiler_params=_params(("parallel", "arbitrary"), pipelined, single),
        name=name,
    )(h, w_in_t)


def _lane_sum_mxu(x):
    ones = jnp.ones((LANES, LANES), BF16)
    hi = x.astype(BF16)
    lo = (x - hi.astype(F32)).astype(BF16)
    return jnp.dot(hi, ones, preferred_element_type=F32) + jnp.dot(lo, ones, preferred_element_type=F32)


def _gqa_prep_kernel(z_ref, g_ref, c_ref, cos_ref, su_ref, sd_ref, o_ref, *, heads_per_step):
    cos, su, sd = cos_ref[...], su_ref[...], sd_ref[...]
    for h in range(heads_per_step):
        sl = slice(h * GQA_HEAD_DIM, (h + 1) * GQA_HEAD_DIM)
        x = z_ref[:, sl]
        ms = _lane_sum_mxu(x * x) * (1.0 / GQA_HEAD_DIM)
        y = x * lax.rsqrt(ms + EPS) * g_ref[:, sl]
        o_ref[:, sl] = (_rope_lanes(y, cos, su, sd, GQA_HEAD_DIM // 4) * c_ref[:, sl]).astype(o_ref.dtype)


def _gqa_prep(z, gains, post_scale, tables, seq, br, heads_per_step, name):
    t, n = z.shape
    bn = heads_per_step * GQA_HEAD_DIM
    s_blocks = seq // br
    tab_spec = pl.BlockSpec((br, LANES), lambda i, j: (i % s_blocks, 0))
    row_spec = pl.BlockSpec((1, bn), lambda i, j: (0, j))
    blocks = [((br, bn), F32), ((br, bn), BF16)] + [((br, LANES), F32)] * 3
    return pl.pallas_call(
        functools.partial(_gqa_prep_kernel, heads_per_step=heads_per_step),
        out_shape=jax.ShapeDtypeStruct((t, n), BF16),
        grid=(t // br, n // bn),
        in_specs=[pl.BlockSpec((br, bn), lambda i, j: (i, j)), row_spec, row_spec,
                  tab_spec, tab_spec, tab_spec],
        out_specs=pl.BlockSpec((br, bn), lambda i, j: (i, j)),
        compiler_params=_params(("parallel", "arbitrary"), blocks, [((br, bn), F32)] * 2),
        name=name,
    )(z, gains, post_scale, *tables)


def _mla_q_kernel(z_ref, g_ref, w_ref, cos_ref, su_ref, sd_ref, o_ref, cq_ref, *, heads_per_step):
    @pl.when(pl.program_id(1) == 0)
    def _():
        z = z_ref[...]
        ms = jnp.mean(z * z, axis=-1, keepdims=True)
        cq_ref[...] = (z * lax.rsqrt(ms + EPS) * g_ref[...]).astype(cq_ref.dtype)

    w = w_ref[...]
    n_chains = o_ref.shape[0] // ROW_CHAIN
    for r in range(n_chains):
        rows = slice(r * ROW_CHAIN, (r + 1) * ROW_CHAIN)
        q = _dot(cq_ref[rows, :], w) * (MLA_SCALE * LOG2_E)
        cos, su, sd = cos_ref[rows, :], su_ref[rows, :], sd_ref[rows, :]
        for h in range(heads_per_step):
            base = h * MLA_QK_PAD
            o_ref[rows, base:base + MLA_NOPE] = q[:, base:base + MLA_NOPE].astype(o_ref.dtype)
            pe = q[:, base + MLA_NOPE:base + MLA_QK_PAD]
            o_ref[rows, base + MLA_NOPE:base + MLA_QK_PAD] = _rope_lanes(
                pe, cos, su, sd, MLA_ROPE // 4).astype(o_ref.dtype)


def _mla_q_proj(z_lat, g_q_a, w_q, tables, seq, bm, heads_per_step, name):
    t = z_lat.shape[0]
    bn = heads_per_step * MLA_QK_PAD
    n = w_q.shape[1]
    s_blocks = seq // bm
    tab_spec = pl.BlockSpec((bm, LANES), lambda i, j: (i % s_blocks, 0))
    blocks = [((bm, Q_LORA), F32), ((Q_LORA, bn), BF16), ((bm, bn), BF16)] + [((bm, LANES), F32)] * 3
    return pl.pallas_call(
        functools.partial(_mla_q_kernel, heads_per_step=heads_per_step),
        out_shape=jax.ShapeDtypeStruct((t, n), BF16),
        grid=(t // bm, n // bn),
        in_specs=[pl.BlockSpec((bm, Q_LORA), lambda i, j: (i, 0)),
                  pl.BlockSpec((1, Q_LORA), lambda i, j: (0, 0)),
                  pl.BlockSpec((Q_LORA, bn), lambda i, j: (0, j)),
                  tab_spec, tab_spec, tab_spec],
        out_specs=pl.BlockSpec((bm, bn), lambda i, j: (i, j)),
        scratch_shapes=[pltpu.VMEM((bm, Q_LORA), BF16)],
        compiler_params=_params(("parallel", "arbitrary"), blocks,
                                [((bm, Q_LORA), BF16), ((bm, bn), F32), ((bm, Q_LORA), F32)]),
        name=name,
    )(z_lat, g_q_a.reshape(1, Q_LORA), w_q, *tables)


def _mla_kv_kernel(z_ref, pe_ref, g_ref, w_ref, cos_ref, su_ref, sd_ref, k_ref, v_ref, ckv_ref, kpe_ref,
                   *, heads_per_step):
    @pl.when(pl.program_id(1) == 0)
    def _():
        z = z_ref[...]
        ms = jnp.mean(z * z, axis=-1, keepdims=True)
        ckv_ref[...] = (z * lax.rsqrt(ms + EPS) * g_ref[...]).astype(ckv_ref.dtype)
        kpe_ref[...] = _rope_lanes(pe_ref[...], cos_ref[...], su_ref[...], sd_ref[...],
                                   MLA_ROPE // 4).astype(kpe_ref.dtype)

    kv = _dot(ckv_ref[...], w_ref[...])
    for h in range(heads_per_step):
        src = h * (MLA_NOPE + MLA_V)
        dst = h * MLA_QK_PAD
        k_ref[:, dst:dst + MLA_NOPE] = kv[:, src:src + MLA_NOPE].astype(k_ref.dtype)
        k_ref[:, dst + MLA_NOPE:dst + MLA_QK_PAD] = kpe_ref[...]
        v_ref[:, h * MLA_V:(h + 1) * MLA_V] = kv[:, src + MLA_NOPE:src + MLA_NOPE + MLA_V].astype(v_ref.dtype)


def _mla_kv_proj(z_lat, z_kpe, g_kv_a, w_kv, tables, seq, bm, heads_per_step, name):
    t = z_lat.shape[0]
    bn = heads_per_step * (MLA_NOPE + MLA_V)
    steps = w_kv.shape[1] // bn
    s_blocks = seq // bm
    tab_spec = pl.BlockSpec((bm, LANES), lambda i, j: (i % s_blocks, 0))
    kva_block = Q_LORA // KV_LORA
    bk, bv = heads_per_step * MLA_QK_PAD, heads_per_step * MLA_V
    blocks = ([((bm, KV_LORA), F32), ((bm, LANES), F32), ((KV_LORA, bn), BF16), ((bm, bk), BF16),
               ((bm, bv), BF16)] + [((bm, LANES), F32)] * 3)
    return pl.pallas_call(
        functools.partial(_mla_kv_kernel, heads_per_step=heads_per_step),
        out_shape=(jax.ShapeDtypeStruct((t, steps * bk), BF16), jax.ShapeDtypeStruct((t, steps * bv), BF16)),
        grid=(t // bm, steps),
        in_specs=[pl.BlockSpec((bm, KV_LORA), lambda i, j: (i, kva_block)),
                  pl.BlockSpec((bm, LANES), lambda i, j: (i, 0)),
                  pl.BlockSpec((1, KV_LORA), lambda i, j: (0, 0)),
                  pl.BlockSpec((KV_LORA, bn), lambda i, j: (0, j)),
                  tab_spec, tab_spec, tab_spec],
        out_specs=(pl.BlockSpec((bm, bk), lambda i, j: (i, j)),
                   pl.BlockSpec((bm, bv), lambda i, j: (i, j))),
        scratch_shapes=[pltpu.VMEM((bm, KV_LORA), BF16), pltpu.VMEM((bm, LANES), BF16)],
        compiler_params=_params(("parallel", "arbitrary"), blocks,
                                [((bm, KV_LORA), BF16), ((bm, bn), F32), ((bm, KV_LORA), F32)]),
        name=name,
    )(z_lat, z_kpe, g_kv_a.reshape(1, KV_LORA), w_kv, *tables)


def _attention_kernel(q_ref, k_ref, v_ref, o_ref, *, bq, q_heads, group, dk, dv):
    for h in range(q_heads):
        g = h // group
        k = k_ref[:, g * dk:(g + 1) * dk]
        v = v_ref[:, g * dv:(g + 1) * dv]
        for qi in range(q_ref.shape[0] // bq):
            rows = slice(qi * bq, (qi + 1) * bq)
            s = lax.dot_general(q_ref[rows, h * dk:(h + 1) * dk], k, (((1,), (1,)), ((), ())),
                                preferred_element_type=F32)
            m = jnp.max(s, axis=-1, keepdims=True)
            p = jnp.exp2(s - m)
            l = jnp.sum(p, axis=-1, keepdims=True)
            o = jnp.dot(p.astype(BF16), v, preferred_element_type=F32)
            o_ref[rows, h * dv:(h + 1) * dv] = (o / l).astype(o_ref.dtype)


def _attention(q, k, v, *, n_q_heads, group, q_heads_per_step, dk, dv, k_head_offset, v_head_offset, bq,
               name):
    b, s, _ = q.shape
    nq = q_heads_per_step
    nk = max(1, nq // group)
    assert (nq % group == 0 or group % nq == 0) and k_head_offset % nk == 0 and v_head_offset % nk == 0
    chains = nq * (s // bq)
    blocks = [((s, nq * dk), BF16), ((s, nk * dk), BF16), ((s, nk * dv), BF16), ((s, nq * dv), BF16)]
    return pl.pallas_call(
        functools.partial(_attention_kernel, bq=bq, q_heads=nq, group=group, dk=dk, dv=dv),
        out_shape=jax.ShapeDtypeStruct((b, s, n_q_heads * dv), BF16),
        grid=(b, n_q_heads // nq),
        in_specs=[pl.BlockSpec((None, s, nq * dk), lambda bi, h: (bi, 0, h)),
                  pl.BlockSpec((None, s, nk * dk), lambda bi, h: (bi, 0, (k_head_offset + h * nq // group) // nk)),
                  pl.BlockSpec((None, s, nk * dv), lambda bi, h: (bi, 0, (v_head_offset + h * nq // group) // nk))],
        out_specs=pl.BlockSpec((None, s, nq * dv), lambda bi, h: (bi, 0, h)),
        compiler_params=_params(("parallel", "arbitrary"), blocks,
                                [((bq, s), F32)] * chains + [((bq, s), BF16)] * chains),
        name=name,
    )(q, k, v)


def _merge_kernel(oa_ref, ob_ref, wa_ref, wb_ref, ga_ref, gb_ref, o_ref):
    pa = _dot(oa_ref[...], wa_ref[...])
    pb = _dot(ob_ref[...], wb_ref[...])
    o_ref[...] = (ga_ref[...].astype(F32) * pa + gb_ref[...].astype(F32) * pb).astype(o_ref.dtype)


def _gated_merge(o_a, o_b, w_a, w_b, gates, gate_block_offset, bm, bn, name):
    t, ka = o_a.shape
    kb = o_b.shape[1]
    n = w_a.shape[1]
    nb = n // bn
    pipelined = [((ka, bn), w_a.dtype), ((kb, bn), w_b.dtype), ((bm, bn), BF16), ((bm, bn), BF16),
                 ((bm, bn), BF16)]
    single = [((bm, ka), BF16), ((bm, kb), BF16)] + [((bm, bn), F32)] * 2
    return pl.pallas_call(
        _merge_kernel,
        out_shape=jax.ShapeDtypeStruct((t, n), BF16),
        grid=(t // bm, nb),
        in_specs=[pl.BlockSpec((bm, ka), lambda i, j: (i, 0), pipeline_mode=_RESIDENT),
                  pl.BlockSpec((bm, kb), lambda i, j: (i, 0), pipeline_mode=_RESIDENT),
                  pl.BlockSpec((ka, bn), lambda i, j: (0, j)),
                  pl.BlockSpec((kb, bn), lambda i, j: (0, j)),
                  pl.BlockSpec((bm, bn), lambda i, j: (i, gate_block_offset + j)),
                  pl.BlockSpec((bm, bn), lambda i, j: (i, gate_block_offset + nb + j))],
        out_specs=pl.BlockSpec((bm, bn), lambda i, j: (i, j)),
        compiler_params=_params(("parallel", "arbitrary"), pipelined, single),
        name=name,
    )(o_a, o_b, w_a, w_b, gates, gates)


def _mm_residual_kernel(a_ref, b_ref, r_ref, o_ref):
    o_ref[...] = r_ref[...] + _dot(a_ref[...], b_ref[...])


def _matmul_residual(a, b, r, bm, bn, name, k_block_index, k_block):
    m = a.shape[0]
    k = k_block
    n = b.shape[1]
    pipelined = [((bm, k), a.dtype), ((k, bn), b.dtype), ((bm, bn), F32), ((bm, bn), F32)]
    single = [((bm, bn), F32)]
    return pl.pallas_call(
        _mm_residual_kernel,
        out_shape=jax.ShapeDtypeStruct((m, n), F32),
        grid=(m // bm, n // bn),
        in_specs=[pl.BlockSpec((bm, k), lambda i, j: (i, k_block_index)),
                  pl.BlockSpec((k, bn), lambda i, j: (k_block_index, j)),
                  pl.BlockSpec((bm, bn), lambda i, j: (i, j))],
        out_specs=pl.BlockSpec((bm, bn), lambda i, j: (i, j)),
        compiler_params=_params(("parallel", "arbitrary"), pipelined, single),
        name=name,
    )(a, b, r)


def _swiglu_kernel(a_ref, wg_ref, wu_ref, wd_ref, o_ref, wd16_ref):
    a = a_ref[...]
    g = _dot(a, wg_ref[...])
    u = _dot(a, wu_ref[...])
    o_ref[...] = (g * _sigmoid(g) * u).astype(o_ref.dtype)

    @pl.when(pl.program_id(0) == 0)
    def _():
        wd16_ref[...] = wd_ref[...].astype(wd16_ref.dtype)


def _swiglu(a, w_gate, w_up, w_down, bm, bn, name):
    t, k = a.shape
    n = w_gate.shape[1]
    nj = n // bn
    d_out = w_down.shape[1]
    chunk = w_down.shape[0] // nj

    def wd_index(i, j):
        return (jnp.where(i == 0, j, nj - 1), 0)

    pipelined = [((k, bn), w_gate.dtype), ((k, bn), w_up.dtype), ((bm, bn), BF16),
                 ((chunk, d_out), w_down.dtype), ((chunk, d_out), BF16)]
    single = [((bm, k), BF16)] + [((bm, bn), F32)] * 3
    return pl.pallas_call(
        _swiglu_kernel,
        out_shape=(jax.ShapeDtypeStruct((t, n), BF16), jax.ShapeDtypeStruct(w_down.shape, BF16)),
        grid=(t // bm, nj),
        in_specs=[pl.BlockSpec((bm, k), lambda i, j: (i, 0), pipeline_mode=_RESIDENT),
                  pl.BlockSpec((k, bn), lambda i, j: (0, j)),
                  pl.BlockSpec((k, bn), lambda i, j: (0, j)),
                  pl.BlockSpec((chunk, d_out), wd_index)],
        out_specs=(pl.BlockSpec((bm, bn), lambda i, j: (i, j)),
                   pl.BlockSpec((chunk, d_out), wd_index)),
        compiler_params=_params(("arbitrary", "arbitrary"), pipelined, single),
        name=name,
    )(a, w_gate, w_up, w_down)


def _layer(xf, batch, seq, g_attn, w_in, g_q_a, w_q_b, g_kv_a, w_kv_b, g_qn, g_kn,
           w_branch_a, w_branch_b, w_o, g_ffn, w_gate, w_up, w_down):
    t, d = xf.shape
    w_q = jnp.pad(w_q_b.reshape(Q_LORA, MLA_HEADS, MLA_NOPE + MLA_ROPE),
                  ((0, 0), (0, 0), (0, MLA_QK_PAD - MLA_NOPE - MLA_ROPE)))
    w_q = w_q.reshape(Q_LORA, MLA_HEADS * MLA_QK_PAD).astype(BF16)
    w_kv = w_kv_b.astype(BF16)
    gqa_gains = jnp.concatenate([jnp.tile(g_qn, GQA_Q_HEADS), jnp.tile(g_kn, GQA_KV_HEADS)]).reshape(1, -1)
    gqa_post = jnp.concatenate([jnp.full((GQA_Q_HEADS * GQA_HEAD_DIM,), GQA_SCALE * LOG2_E, F32),
                                jnp.ones((GQA_KV_HEADS * GQA_HEAD_DIM,), F32)]).reshape(1, -1)
    mla_tables = _axial_tables(seq, MLA_ROPE)
    gqa_tables = _axial_tables(seq, GQA_HEAD_DIM)

    h = _rmsnorm(xf, g_attn, BF16, "norm_attn")
    w_in_t = w_in.T
    z_lat, z_kpe = _in_proj_latent(h, w_in_t, 1024, "in_proj_latent")
    row = LATENT_COLS + MLA_ROPE
    n_qk = (GQA_Q_HEADS + GQA_KV_HEADS) * GQA_HEAD_DIM
    n_v = GQA_KV_HEADS * GQA_HEAD_DIM
    z_qk = _in_proj_rows(_in_proj_cast_kernel, h, w_in_t, row, n_qk, F32, 2048, "in_proj_qk")
    v_b = _in_proj_rows(_in_proj_cast_kernel, h, w_in_t, row + n_qk, n_v, BF16, 1024, "in_proj_v")
    gates = _in_proj_rows(_in_proj_sigmoid_kernel, h, w_in_t, row + n_qk + n_v, 2 * d, BF16, 2048,
                          "in_proj_gate")
    qk_b = _gqa_prep(z_qk, gqa_gains, gqa_post, gqa_tables, seq, 1024, 4, "gqa_prep")

    q_a = _mla_q_proj(z_lat, g_q_a, w_q, mla_tables, seq, 1024, 4, "mla_q_proj")
    k_a, v_a = _mla_kv_proj(z_lat, z_kpe, g_kv_a, w_kv, mla_tables, seq, 1024, 8, "mla_kv_proj")
    o_a = _attention(q_a.reshape(batch, seq, -1), k_a.reshape(batch, seq, -1), v_a.reshape(batch, seq, -1),
                     n_q_heads=MLA_HEADS, group=1, q_heads_per_step=2, dk=MLA_QK_PAD, dv=MLA_V,
                     k_head_offset=0, v_head_offset=0, bq=512, name="mla_attention")

    qk_b3 = qk_b.reshape(batch, seq, -1)
    o_b = _attention(qk_b3, qk_b3, v_b.reshape(batch, seq, -1), n_q_heads=GQA_Q_HEADS,
                     group=GQA_Q_HEADS // GQA_KV_HEADS, q_heads_per_step=2, dk=GQA_HEAD_DIM,
                     dv=GQA_HEAD_DIM, k_head_offset=GQA_Q_HEADS, v_head_offset=0, bq=512,
                     name="gqa_attention")

    m = _gated_merge(o_a.reshape(t, -1), o_b.reshape(t, -1), w_branch_a, w_branch_b, gates,
                     0, 2048, 512, "gated_merge")
    x1 = _matmul_residual(m, w_o, xf, 1024, 512, "out_proj", k_block_index=0, k_block=d)

    h2 = _rmsnorm(x1, g_ffn, BF16, "norm_ffn")
    act, wd = _swiglu(h2, w_gate, w_up, w_down, 2048, 256, "ffn_gate_up")
    k_half = act.shape[1] // 2
    y = _matmul_residual(act, wd, x1, 1024, 512, "ffn_down_lo", k_block_index=0, k_block=k_half)
    return _matmul_residual(act, wd, y, 1024, 512, "ffn_down_hi", k_block_index=1, k_block=k_half)


def kernel(x, g_attn, w_in, g_q_a, w_q_b, g_kv_a, w_kv_b, g_qn, g_kn, w_branch_a, w_branch_b, w_o,
           g_ffn, w_gate, w_up, w_down, g_final):
    batch, seq, d = x.shape
    xf = x.reshape(batch * seq, d)
    for l in range(g_attn.shape[0]):
        xf = _layer(xf, batch, seq, g_attn[l], w_in[l], g_q_a[l], w_q_b[l], g_kv_a[l], w_kv_b[l],
                    g_qn[l], g_kn[l], w_branch_a[l], w_branch_b[l], w_o[l], g_ffn[l],
                    w_gate[l], w_up[l], w_down[l])
    return _rmsnorm(xf, g_final, F32, "norm_final").reshape(batch, seq, d)
```

```python
import functools

import jax
import jax.numpy as jnp
import numpy as np
from jax import lax
from jax.experimental import pallas as pl
from jax.experimental.pallas import tpu as pltpu

GRID_W = 64
ROPE_THETA = 10000.0
EPS = 1e-6

MLA_HEADS = 16
MLA_NOPE = 128
MLA_ROPE = 64
MLA_V = 128
MLA_QK_PAD = 256
Q_LORA = 1024
KV_LORA = 512
MLA_SCALE = (MLA_NOPE + MLA_ROPE) ** -0.5

GQA_Q_HEADS = 16
GQA_KV_HEADS = 4
GQA_HEAD_DIM = 128
GQA_SCALE = GQA_HEAD_DIM ** -0.5
LOG2_E = 1.4426950408889634

LANES = 128
SUBLANES = 8
V7X_VMEM_BYTES = 64 * 1024 * 1024
VMEM_CAP_BYTES = V7X_VMEM_BYTES - 6 * 1024 * 1024

IN_BLOCK = 512
ROW_CHAIN = 256
LATENT_COLS = Q_LORA + KV_LORA

BF16 = jnp.bfloat16
F32 = jnp.float32

_RESIDENT = pl.Buffered(1)
RESIDENT_MIN_BYTES = 16 * 1024 * 1024


def _nbytes(shape, dtype):
    return int(np.prod(shape)) * jnp.dtype(dtype).itemsize


def _params(semantics, pipelined_blocks, single_blocks=()):
    est = 2 * sum(_nbytes(s, d) for s, d in pipelined_blocks)
    est += sum(_nbytes(s, d) for s, d in single_blocks)
    limit = min(VMEM_CAP_BYTES, int(est * 1.25) + (4 << 20))
    return pltpu.CompilerParams(dimension_semantics=semantics, vmem_limit_bytes=limit)


def _dot(a, b):
    return jnp.dot(a, b.astype(BF16), preferred_element_type=F32)


def _sigmoid(x):
    return 0.5 * jnp.tanh(0.5 * x) + 0.5


def _rmsnorm_kernel(x_ref, g_ref, o_ref):
    x = x_ref[...]
    ms = jnp.mean(x * x, axis=-1, keepdims=True)
    o_ref[...] = (x * lax.rsqrt(ms + EPS) * g_ref[...]).astype(o_ref.dtype)


def _rmsnorm(x, g, out_dtype, name, block_rows=256):
    t, d = x.shape
    blocks = [((block_rows, d), F32), ((block_rows, d), out_dtype)]
    return pl.pallas_call(
        _rmsnorm_kernel,
        out_shape=jax.ShapeDtypeStruct((t, d), out_dtype),
        grid=(t // block_rows,),
        in_specs=[pl.BlockSpec((block_rows, d), lambda i: (i, 0)),
                  pl.BlockSpec((1, d), lambda i: (0, 0))],
        out_specs=pl.BlockSpec((block_rows, d), lambda i: (i, 0)),
        compiler_params=_params(("parallel",), blocks, [((block_rows, d), F32)] * 2),
        name=name,
    )(x, g.reshape(1, d))


def _rope_table(pos, dim):
    inv = ROPE_THETA ** (-jnp.arange(0, dim, 2, dtype=F32) / dim)
    ang = pos.astype(F32)[:, None] * inv[None, :]
    ang = jnp.concatenate([ang, ang], axis=-1)
    return jnp.cos(ang), jnp.sin(ang)


def _axial_tables(seq, rot_dim):
    rows = seq // GRID_W
    row_idx = jnp.repeat(jnp.arange(rows, dtype=jnp.int32), GRID_W)
    col_idx = jnp.tile(jnp.arange(GRID_W, dtype=jnp.int32), rows)
    half = rot_dim // 2
    cos_r, sin_r = _rope_table(row_idx, half)
    cos_c, sin_c = _rope_table(col_idx, half)
    cos = jnp.concatenate([cos_r, cos_c], axis=-1)
    sin = jnp.concatenate([sin_r, sin_c], axis=-1)
    first = (jnp.arange(rot_dim) % half) < (half // 2)
    sin_up = jnp.where(first[None, :], -sin, 0.0)
    sin_dn = jnp.where(first[None, :], 0.0, sin)
    pad = ((0, 0), (0, LANES - rot_dim))
    return tuple(jnp.pad(t, pad) for t in (cos, sin_up, sin_dn))


def _rope_lanes(x, cos, sin_up, sin_dn, quarter):
    up = pltpu.roll(x, LANES - quarter, 1)
    dn = pltpu.roll(x, quarter, 1)
    return x * cos + up * sin_up + dn * sin_dn


def _dot_nt(a, bt):
    return lax.dot_general(a, bt.astype(BF16), (((1,), (1,)), ((), ())), preferred_element_type=F32)


def _in_proj_latent_kernel(h_ref, wt_ref, wtk_ref, z_ref, zk_ref):
    h = h_ref[...]
    z_ref[...] = _dot_nt(h, wt_ref[...])

    @pl.when(pl.program_id(1) == 0)
    def _():
        lane = lax.broadcasted_iota(jnp.int32, (1, LANES), 1)
        zk_ref[...] = jnp.where(lane < MLA_ROPE, _dot_nt(h, wtk_ref[...]), 0.0)


def _in_proj_latent(h, w_in_t, bm, name):
    t, k = h.shape
    nb = LATENT_COLS // IN_BLOCK
    blocks = [((bm, k), BF16), ((IN_BLOCK, k), F32), ((LANES, k), F32), ((bm, IN_BLOCK), F32),
              ((bm, LANES), F32)]
    return pl.pallas_call(
        _in_proj_latent_kernel,
        out_shape=(jax.ShapeDtypeStruct((t, LATENT_COLS), F32), jax.ShapeDtypeStruct((t, LANES), F32)),
        grid=(t // bm, nb),
        in_specs=[pl.BlockSpec((bm, k), lambda i, j: (i, 0)),
                  pl.BlockSpec((IN_BLOCK, k), lambda i, j: (j, 0)),
                  pl.BlockSpec((LANES, k), lambda i, j: (LATENT_COLS // LANES, 0))],
        out_specs=(pl.BlockSpec((bm, IN_BLOCK), lambda i, j: (i, j)),
                   pl.BlockSpec((bm, LANES), lambda i, j: (i, 0))),
        compiler_params=_params(("parallel", "arbitrary"), blocks, [((bm, IN_BLOCK), F32)]),
        name=name,
    )(h, w_in_t, w_in_t)


def _in_proj_cast_kernel(h_ref, wt_ref, o_ref):
    o_ref[...] = _dot_nt(h_ref[...], wt_ref[...]).astype(o_ref.dtype)


def _in_proj_sigmoid_kernel(h_ref, wt_ref, o_ref):
    o_ref[...] = _sigmoid(_dot_nt(h_ref[...], wt_ref[...])).astype(o_ref.dtype)


def _in_proj_rows(body, h, w_in_t, first_row, n_rows, out_dtype, bm, name):
    t, k = h.shape
    resident_h = _nbytes((bm, k), BF16) >= RESIDENT_MIN_BYTES
    pipelined = [((IN_BLOCK, k), F32), ((bm, IN_BLOCK), out_dtype)] + ([] if resident_h else [((bm, k), BF16)])
    single = [((bm, IN_BLOCK), F32), ((IN_BLOCK, k), BF16)] + ([((bm, k), BF16)] if resident_h else [])
    return pl.pallas_call(
        body,
        out_shape=jax.ShapeDtypeStruct((t, n_rows), out_dtype),
        grid=(t // bm, n_rows // IN_BLOCK),
        in_specs=[pl.BlockSpec((bm, k), lambda i, j: (i, 0), pipeline_mode=_RESIDENT if resident_h else None),
                  pl.BlockSpec((pl.Element(IN_BLOCK), pl.Element(k)),
                               lambda i, j: ((first_row // SUBLANES + j * (IN_BLOCK // SUBLANES)) * SUBLANES, 0))],
        out_specs=pl.BlockSpec((bm, IN_BLOCK), lambda i, j: (i, j)),
        compiler_params=_params(("parallel", "arbitrary"), pipelined, single),
        name=name,
    )(h, w_in_t)


def _lane_sum_mxu(x):
    ones = jnp.ones((LANES, LANES), BF16)
    hi = x.astype(BF16)
    lo = (x - hi.astype(F32)).astype(BF16)
    return jnp.dot(hi, ones, preferred_element_type=F32) + jnp.dot(lo, ones, preferred_element_type=F32)


def _gqa_prep_kernel(z_ref, g_ref, c_ref, cos_ref, su_ref, sd_ref, o_ref, *, heads_per_step):
    cos, su, sd = cos_ref[...], su_ref[...], sd_ref[...]
    for h in range(heads_per_step):
        sl = slice(h * GQA_HEAD_DIM, (h + 1) * GQA_HEAD_DIM)
        x = z_ref[:, sl]
        ms = _lane_sum_mxu(x * x) * (1.0 / GQA_HEAD_DIM)
        y = x * lax.rsqrt(ms + EPS) * g_ref[:, sl]
        o_ref[:, sl] = (_rope_lanes(y, cos, su, sd, GQA_HEAD_DIM // 4) * c_ref[:, sl]).astype(o_ref.dtype)


def _gqa_prep(z, gains, post_scale, tables, seq, br, heads_per_step, name):
    t, n = z.shape
    bn = heads_per_step * GQA_HEAD_DIM
    s_blocks = seq // br
    tab_spec = pl.BlockSpec((br, LANES), lambda i, j: (i % s_blocks, 0))
    row_spec = pl.BlockSpec((1, bn), lambda i, j: (0, j))
    blocks = [((br, bn), F32), ((br, bn), BF16)] + [((br, LANES), F32)] * 3
    return pl.pallas_call(
        functools.partial(_gqa_prep_kernel, heads_per_step=heads_per_step),
        out_shape=jax.ShapeDtypeStruct((t, n), BF16),
        grid=(t // br, n // bn),
        in_specs=[pl.BlockSpec((br, bn), lambda i, j: (i, j)), row_spec, row_spec,
                  tab_spec, tab_spec, tab_spec],
        out_specs=pl.BlockSpec((br, bn), lambda i, j: (i, j)),
        compiler_params=_params(("parallel", "arbitrary"), blocks, [((br, bn), F32)] * 2),
        name=name,
    )(z, gains, post_scale, *tables)


def _mla_q_kernel(z_ref, g_ref, w_ref, cos_ref, su_ref, sd_ref, o_ref, cq_ref, *, heads_per_step):
    @pl.when(pl.program_id(1) == 0)
    def _():
        z = z_ref[...]
        ms = jnp.mean(z * z, axis=-1, keepdims=True)
        cq_ref[...] = (z * lax.rsqrt(ms + EPS) * g_ref[...]).astype(cq_ref.dtype)

    w = w_ref[...]
    n_chains = o_ref.shape[0] // ROW_CHAIN
    for r in range(n_chains):
        rows = slice(r * ROW_CHAIN, (r + 1) * ROW_CHAIN)
        q = _dot(cq_ref[rows, :], w) * (MLA_SCALE * LOG2_E)
        cos, su, sd = cos_ref[rows, :], su_ref[rows, :], sd_ref[rows, :]
        for h in range(heads_per_step):
            base = h * MLA_QK_PAD
            o_ref[rows, base:base + MLA_NOPE] = q[:, base:base + MLA_NOPE].astype(o_ref.dtype)
            pe = q[:, base + MLA_NOPE:base + MLA_QK_PAD]
            o_ref[rows, base + MLA_NOPE:base + MLA_QK_PAD] = _rope_lanes(
                pe, cos, su, sd, MLA_ROPE // 4).astype(o_ref.dtype)


def _mla_q_proj(z_lat, g_q_a, w_q, tables, seq, bm, heads_per_step, name):
    t = z_lat.shape[0]
    bn = heads_per_step * MLA_QK_PAD
    n = w_q.shape[1]
    s_blocks = seq // bm
    tab_spec = pl.BlockSpec((bm, LANES), lambda i, j: (i % s_blocks, 0))
    blocks = [((bm, Q_LORA), F32), ((Q_LORA, bn), BF16), ((bm, bn), BF16)] + [((bm, LANES), F32)] * 3
    return pl.pallas_call(
        functools.partial(_mla_q_kernel, heads_per_step=heads_per_step),
        out_shape=jax.ShapeDtypeStruct((t, n), BF16),
        grid=(t // bm, n // bn),
        in_specs=[pl.BlockSpec((bm, Q_LORA), lambda i, j: (i, 0)),
                  pl.BlockSpec((1, Q_LORA), lambda i, j: (0, 0)),
                  pl.BlockSpec((Q_LORA, bn), lambda i, j: (0, j)),
                  tab_spec, tab_spec, tab_spec],
        out_specs=pl.BlockSpec((bm, bn), lambda i, j: (i, j)),
        scratch_shapes=[pltpu.VMEM((bm, Q_LORA), BF16)],
        compiler_params=_params(("parallel", "arbitrary"), blocks,
                                [((bm, Q_LORA), BF16), ((bm, bn), F32), ((bm, Q_LORA), F32)]),
        name=name,
    )(z_lat, g_q_a.reshape(1, Q_LORA), w_q, *tables)


def _mla_kv_kernel(z_ref, pe_ref, g_ref, w_ref, cos_ref, su_ref, sd_ref, k_ref, v_ref, ckv_ref, kpe_ref,
                   *, heads_per_step):
    @pl.when(pl.program_id(1) == 0)
    def _():
        z = z_ref[...]
        ms = jnp.mean(z * z, axis=-1, keepdims=True)
        ckv_ref[...] = (z * lax.rsqrt(ms + EPS) * g_ref[...]).astype(ckv_ref.dtype)
        kpe_ref[...] = _rope_lanes(pe_ref[...], cos_ref[...], su_ref[...], sd_ref[...],
                                   MLA_ROPE // 4).astype(kpe_ref.dtype)

    kv = _dot(ckv_ref[...], w_ref[...])
    for h in range(heads_per_step):
        src = h * (MLA_NOPE + MLA_V)
        dst = h * MLA_QK_PAD
        k_ref[:, dst:dst + MLA_NOPE] = kv[:, src:src + MLA_NOPE].astype(k_ref.dtype)
        k_ref[:, dst + MLA_NOPE:dst + MLA_QK_PAD] = kpe_ref[...]
        v_ref[:, h * MLA_V:(h + 1) * MLA_V] = kv[:, src + MLA_NOPE:src + MLA_NOPE + MLA_V].astype(v_ref.dtype)


def _mla_kv_proj(z_lat, z_kpe, g_kv_a, w_kv, tables, seq, bm, heads_per_step, name):
    t = z_lat.shape[0]
    bn = heads_per_step * (MLA_NOPE + MLA_V)
    steps = w_kv.shape[1] // bn
    s_blocks = seq // bm
    tab_spec = pl.BlockSpec((bm, LANES), lambda i, j: (i % s_blocks, 0))
    kva_block = Q_LORA // KV_LORA
    bk, bv = heads_per_step * MLA_QK_PAD, heads_per_step * MLA_V
    blocks = ([((bm, KV_LORA), F32), ((bm, LANES), F32), ((KV_LORA, bn), BF16), ((bm, bk), BF16),
               ((bm, bv), BF16)] + [((bm, LANES), F32)] * 3)
    return pl.pallas_call(
        functools.partial(_mla_kv_kernel, heads_per_step=heads_per_step),
        out_shape=(jax.ShapeDtypeStruct((t, steps * bk), BF16), jax.ShapeDtypeStruct((t, steps * bv), BF16)),
        grid=(t // bm, steps),
        in_specs=[pl.BlockSpec((bm, KV_LORA), lambda i, j: (i, kva_block)),
                  pl.BlockSpec((bm, LANES), lambda i, j: (i, 0)),
                  pl.BlockSpec((1, KV_LORA), lambda i, j: (0, 0)),
                  pl.BlockSpec((KV_LORA, bn), lambda i, j: (0, j)),
                  tab_spec, tab_spec, tab_spec],
        out_specs=(pl.BlockSpec((bm, bk), lambda i, j: (i, j)),
                   pl.BlockSpec((bm, bv), lambda i, j: (i, j))),
        scratch_shapes=[pltpu.VMEM((bm, KV_LORA), BF16), pltpu.VMEM((bm, LANES), BF16)],
        compiler_params=_params(("parallel", "arbitrary"), blocks,
                                [((bm, KV_LORA), BF16), ((bm, bn), F32), ((bm, KV_LORA), F32)]),
        name=name,
    )(z_lat, z_kpe, g_kv_a.reshape(1, KV_LORA), w_kv, *tables)


def _attention_kernel(q_ref, k_ref, v_ref, *refs, bq, q_heads, group, dk, dv, n_side):
    side_in, o_ref, side_out = refs[:n_side], refs[n_side], refs[n_side + 1:]
    for w_ref, w16_ref in zip(side_in, side_out):
        w16_ref[...] = w_ref[...].astype(w16_ref.dtype)
    for h in range(q_heads):
        g = h // group
        k = k_ref[:, g * dk:(g + 1) * dk]
        v = v_ref[:, g * dv:(g + 1) * dv]
        for qi in range(q_ref.shape[0] // bq):
            rows = slice(qi * bq, (qi + 1) * bq)
            s = lax.dot_general(q_ref[rows, h * dk:(h + 1) * dk], k, (((1,), (1,)), ((), ())),
                                preferred_element_type=F32)
            m = jnp.max(s, axis=-1, keepdims=True)
            p = jnp.exp2(s - m)
            l = jnp.sum(p, axis=-1, keepdims=True)
            o = jnp.dot(p.astype(BF16), v, preferred_element_type=F32)
            o_ref[rows, h * dv:(h + 1) * dv] = (o / l).astype(o_ref.dtype)


def _attention(q, k, v, *, n_q_heads, group, q_heads_per_step, dk, dv, k_head_offset, v_head_offset, bq,
               name, side_weights=()):
    b, s, _ = q.shape
    nq = q_heads_per_step
    nk = max(1, nq // group)
    assert (nq % group == 0 or group % nq == 0) and k_head_offset % nk == 0 and v_head_offset % nk == 0
    n_h = n_q_heads // nq
    chains = nq * (s // bq)
    blocks = [((s, nq * dk), BF16), ((s, nk * dk), BF16), ((s, nk * dv), BF16), ((s, nq * dv), BF16)]
    side_specs, side_shapes = [], []
    for w in side_weights:
        chunk = w.shape[0] // (b * n_h)
        assert chunk * b * n_h == w.shape[0]
        side_specs.append(pl.BlockSpec((chunk, w.shape[1]), lambda bi, h: (bi * n_h + h, 0)))
        side_shapes.append(jax.ShapeDtypeStruct(w.shape, BF16))
        blocks += [((chunk, w.shape[1]), F32), ((chunk, w.shape[1]), BF16)]
    out = pl.pallas_call(
        functools.partial(_attention_kernel, bq=bq, q_heads=nq, group=group, dk=dk, dv=dv,
                          n_side=len(side_weights)),
        out_shape=[jax.ShapeDtypeStruct((b, s, n_q_heads * dv), BF16)] + side_shapes,
        grid=(b, n_h),
        in_specs=[pl.BlockSpec((None, s, nq * dk), lambda bi, h: (bi, 0, h)),
                  pl.BlockSpec((None, s, nk * dk), lambda bi, h: (bi, 0, (k_head_offset + h * nq // group) // nk)),
                  pl.BlockSpec((None, s, nk * dv), lambda bi, h: (bi, 0, (v_head_offset + h * nq // group) // nk))]
                 + side_specs,
        out_specs=[pl.BlockSpec((None, s, nq * dv), lambda bi, h: (bi, 0, h))] + side_specs,
        compiler_params=_params(("parallel", "arbitrary"), blocks,
                                [((bq, s), F32)] * chains + [((bq, s), BF16)] * chains),
        name=name,
    )(q, k, v, *side_weights)
    return out[0], out[1:]


def _merge_kernel(oa_ref, ob_ref, wa_ref, wb_ref, ga_ref, gb_ref, o_ref):
    pa = _dot(oa_ref[...], wa_ref[...])
    pb = _dot(ob_ref[...], wb_ref[...])
    o_ref[...] = (ga_ref[...].astype(F32) * pa + gb_ref[...].astype(F32) * pb).astype(o_ref.dtype)


def _gated_merge(o_a, o_b, w_a, w_b, gates, gate_block_offset, bm, bn, name):
    t, ka = o_a.shape
    kb = o_b.shape[1]
    n = w_a.shape[1]
    nb = n // bn
    pipelined = [((ka, bn), w_a.dtype), ((kb, bn), w_b.dtype), ((bm, bn), BF16), ((bm, bn), BF16),
                 ((bm, bn), BF16)]
    single = [((bm, ka), BF16), ((bm, kb), BF16)] + [((bm, bn), F32)] * 2
    return pl.pallas_call(
        _merge_kernel,
        out_shape=jax.ShapeDtypeStruct((t, n), BF16),
        grid=(t // bm, nb),
        in_specs=[pl.BlockSpec((bm, ka), lambda i, j: (i, 0), pipeline_mode=_RESIDENT),
                  pl.BlockSpec((bm, kb), lambda i, j: (i, 0), pipeline_mode=_RESIDENT),
                  pl.BlockSpec((ka, bn), lambda i, j: (0, j)),
                  pl.BlockSpec((kb, bn), lambda i, j: (0, j)),
                  pl.BlockSpec((bm, bn), lambda i, j: (i, gate_block_offset + j)),
                  pl.BlockSpec((bm, bn), lambda i, j: (i, gate_block_offset + nb + j))],
        out_specs=pl.BlockSpec((bm, bn), lambda i, j: (i, j)),
        compiler_params=_params(("parallel", "arbitrary"), pipelined, single),
        name=name,
    )(o_a, o_b, w_a, w_b, gates, gates)


def _mm_residual_kernel(a_ref, b_ref, r_ref, o_ref):
    o_ref[...] = r_ref[...] + _dot(a_ref[...], b_ref[...])


def _matmul_residual(a, b, r, bm, bn, name, k_block_index, k_block):
    m = a.shape[0]
    k = k_block
    n = b.shape[1]
    pipelined = [((bm, k), a.dtype), ((k, bn), b.dtype), ((bm, bn), F32), ((bm, bn), F32)]
    single = [((bm, bn), F32)]
    return pl.pallas_call(
        _mm_residual_kernel,
        out_shape=jax.ShapeDtypeStruct((m, n), F32),
        grid=(m // bm, n // bn),
        in_specs=[pl.BlockSpec((bm, k), lambda i, j: (i, k_block_index)),
                  pl.BlockSpec((k, bn), lambda i, j: (k_block_index, j)),
                  pl.BlockSpec((bm, bn), lambda i, j: (i, j))],
        out_specs=pl.BlockSpec((bm, bn), lambda i, j: (i, j)),
        compiler_params=_params(("parallel", "arbitrary"), pipelined, single),
        name=name,
    )(a, b, r)


def _swiglu_kernel(a_ref, wg_ref, wu_ref, wd_ref, o_ref, wd16_ref):
    a = a_ref[...]
    g = _dot(a, wg_ref[...])
    u = _dot(a, wu_ref[...])
    o_ref[...] = (g * _sigmoid(g) * u).astype(o_ref.dtype)

    @pl.when(pl.program_id(0) == 0)
    def _():
        wd16_ref[...] = wd_ref[...].astype(wd16_ref.dtype)


def _swiglu(a, w_gate, w_up, w_down, bm, bn, name):
    t, k = a.shape
    n = w_gate.shape[1]
    nj = n // bn
    d_out = w_down.shape[1]
    chunk = w_down.shape[0] // nj

    def wd_index(i, j):
        return (jnp.where(i == 0, j, nj - 1), 0)

    pipelined = [((k, bn), w_gate.dtype), ((k, bn), w_up.dtype), ((bm, bn), BF16),
                 ((chunk, d_out), w_down.dtype), ((chunk, d_out), BF16)]
    single = [((bm, k), BF16)] + [((bm, bn), F32)] * 3
    return pl.pallas_call(
        _swiglu_kernel,
        out_shape=(jax.ShapeDtypeStruct((t, n), BF16), jax.ShapeDtypeStruct(w_down.shape, BF16)),
        grid=(t // bm, nj),
        in_specs=[pl.BlockSpec((bm, k), lambda i, j: (i, 0), pipeline_mode=_RESIDENT),
                  pl.BlockSpec((k, bn), lambda i, j: (0, j)),
                  pl.BlockSpec((k, bn), lambda i, j: (0, j)),
                  pl.BlockSpec((chunk, d_out), wd_index)],
        out_specs=(pl.BlockSpec((bm, bn), lambda i, j: (i, j)),
                   pl.BlockSpec((chunk, d_out), wd_index)),
        compiler_params=_params(("arbitrary", "arbitrary"), pipelined, single),
        name=name,
    )(a, w_gate, w_up, w_down)


def _layer(xf, batch, seq, g_attn, w_in, g_q_a, w_q_b, g_kv_a, w_kv_b, g_qn, g_kn,
           w_branch_a, w_branch_b, w_o, g_ffn, w_gate, w_up, w_down):
    t, d = xf.shape
    w_q = jnp.pad(w_q_b.reshape(Q_LORA, MLA_HEADS, MLA_NOPE + MLA_ROPE),
                  ((0, 0), (0, 0), (0, MLA_QK_PAD - MLA_NOPE - MLA_ROPE)))
    w_q = w_q.reshape(Q_LORA, MLA_HEADS * MLA_QK_PAD).astype(BF16)
    w_kv = w_kv_b.astype(BF16)
    gqa_gains = jnp.concatenate([jnp.tile(g_qn, GQA_Q_HEADS), jnp.tile(g_kn, GQA_KV_HEADS)]).reshape(1, -1)
    gqa_post = jnp.concatenate([jnp.full((GQA_Q_HEADS * GQA_HEAD_DIM,), GQA_SCALE * LOG2_E, F32),
                                jnp.ones((GQA_KV_HEADS * GQA_HEAD_DIM,), F32)]).reshape(1, -1)
    mla_tables = _axial_tables(seq, MLA_ROPE)
    gqa_tables = _axial_tables(seq, GQA_HEAD_DIM)

    h = _rmsnorm(xf, g_attn, BF16, "norm_attn")
    w_in_t = w_in.T
    z_lat, z_kpe = _in_proj_latent(h, w_in_t, 1024, "in_proj_latent")
    row = LATENT_COLS + MLA_ROPE
    n_qk = (GQA_Q_HEADS + GQA_KV_HEADS) * GQA_HEAD_DIM
    n_v = GQA_KV_HEADS * GQA_HEAD_DIM
    z_qk = _in_proj_rows(_in_proj_cast_kernel, h, w_in_t, row, n_qk, F32, 2048, "in_proj_qk")
    v_b = _in_proj_rows(_in_proj_cast_kernel, h, w_in_t, row + n_qk, n_v, BF16, 1024, "in_proj_v")
    gates = _in_proj_rows(_in_proj_sigmoid_kernel, h, w_in_t, row + n_qk + n_v, 2 * d, BF16, 2048,
                          "in_proj_gate")
    qk_b = _gqa_prep(z_qk, gqa_gains, gqa_post, gqa_tables, seq, 1024, 4, "gqa_prep")

    q_a = _mla_q_proj(z_lat, g_q_a, w_q, mla_tables, seq, 1024, 4, "mla_q_proj")
    k_a, v_a = _mla_kv_proj(z_lat, z_kpe, g_kv_a, w_kv, mla_tables, seq, 1024, 8, "mla_kv_proj")
    o_a, (w_o16,) = _attention(
        q_a.reshape(batch, seq, -1), k_a.reshape(batch, seq, -1), v_a.reshape(batch, seq, -1),
        n_q_heads=MLA_HEADS, group=1, q_heads_per_step=2, dk=MLA_QK_PAD, dv=MLA_V,
        k_head_offset=0, v_head_offset=0, bq=512, name="mla_attention", side_weights=(w_o,))

    qk_b3 = qk_b.reshape(batch, seq, -1)
    o_b, (w_a16, w_b16) = _attention(
        qk_b3, qk_b3, v_b.reshape(batch, seq, -1), n_q_heads=GQA_Q_HEADS,
        group=GQA_Q_HEADS // GQA_KV_HEADS, q_heads_per_step=2, dk=GQA_HEAD_DIM, dv=GQA_HEAD_DIM,
        k_head_offset=GQA_Q_HEADS, v_head_offset=0, bq=512, name="gqa_attention",
        side_weights=(w_branch_a, w_branch_b))

    m = _gated_merge(o_a.reshape(t, -1), o_b.reshape(t, -1), w_a16, w_b16, gates, 0, 2048, 512,
                     "gated_merge")
    x1 = _matmul_residual(m, w_o16, xf, 1024, 1024, "out_proj", k_block_index=0, k_block=d)

    h2 = _rmsnorm(x1, g_ffn, BF16, "norm_ffn")
    act, wd = _swiglu(h2, w_gate, w_up, w_down, 2048, 256, "ffn_gate_up")
    k_half = act.shape[1] // 2
    y = _matmul_residual(act, wd, x1, 1024, 512, "ffn_down_lo", k_block_index=0, k_block=k_half)
    return _matmul_residual(act, wd, y, 1024, 512, "ffn_down_hi", k_block_index=1, k_block=k_half)


def kernel(x, g_attn, w_in, g_q_a, w_q_b, g_kv_a, w_kv_b, g_qn, g_kn, w_branch_a, w_branch_b, w_o,
           g_ffn, w_gate, w_up, w_down, g_final):
    batch, seq, d = x.shape
    xf = x.reshape(batch * seq, d)
    for l in range(g_attn.shape[0]):
        xf = _layer(xf, batch, seq, g_attn[l], w_in[l], g_q_a[l], w_q_b[l], g_kv_a[l], w_kv_b[l],
                    g_qn[l], g_kn[l], w_branch_a[l], w_branch_b[l], w_o[l], g_ffn[l],
                    w_gate[l], w_up[l], w_down[l])
    return _rmsnorm(xf, g_final, F32, "norm_final").reshape(batch, seq, d)
```

```python
import functools

import jax
import jax.numpy as jnp
import numpy as np
from jax import lax
from jax.experimental import pallas as pl
from jax.experimental.pallas import tpu as pltpu

GRID_W = 64
ROPE_THETA = 10000.0
EPS = 1e-6

MLA_HEADS = 16
MLA_NOPE = 128
MLA_ROPE = 64
MLA_V = 128
MLA_QK_PAD = 256
Q_LORA = 1024
KV_LORA = 512
MLA_SCALE = (MLA_NOPE + MLA_ROPE) ** -0.5

GQA_Q_HEADS = 16
GQA_KV_HEADS = 4
GQA_HEAD_DIM = 128
GQA_SCALE = GQA_HEAD_DIM ** -0.5
LOG2_E = 1.4426950408889634

LANES = 128
SUBLANES = 8
V7X_VMEM_BYTES = 64 * 1024 * 1024
VMEM_CAP_BYTES = V7X_VMEM_BYTES - 6 * 1024 * 1024

IN_BLOCK = 512
ROW_CHAIN = 256
LATENT_COLS = Q_LORA + KV_LORA

BF16 = jnp.bfloat16
F32 = jnp.float32

_RESIDENT = pl.Buffered(1)
RESIDENT_MIN_BYTES = 16 * 1024 * 1024


def _nbytes(shape, dtype):
    return int(np.prod(shape)) * jnp.dtype(dtype).itemsize


def _params(semantics, pipelined_blocks, single_blocks=()):
    est = 2 * sum(_nbytes(s, d) for s, d in pipelined_blocks)
    est += sum(_nbytes(s, d) for s, d in single_blocks)
    limit = min(VMEM_CAP_BYTES, int(est * 1.25) + (4 << 20))
    return pltpu.CompilerParams(dimension_semantics=semantics, vmem_limit_bytes=limit)


def _dot(a, b):
    return jnp.dot(a, b.astype(BF16), preferred_element_type=F32)


def _sigmoid(x):
    return 0.5 * jnp.tanh(0.5 * x) + 0.5


def _rmsnorm_kernel(x_ref, g_ref, o_ref):
    x = x_ref[...]
    ms = jnp.mean(x * x, axis=-1, keepdims=True)
    o_ref[...] = (x * lax.rsqrt(ms + EPS) * g_ref[...]).astype(o_ref.dtype)


def _rmsnorm(x, g, out_dtype, name, block_rows=256):
    t, d = x.shape
    blocks = [((block_rows, d), F32), ((block_rows, d), out_dtype)]
    return pl.pallas_call(
        _rmsnorm_kernel,
        out_shape=jax.ShapeDtypeStruct((t, d), out_dtype),
        grid=(t // block_rows,),
        in_specs=[pl.BlockSpec((block_rows, d), lambda i: (i, 0)),
                  pl.BlockSpec((1, d), lambda i: (0, 0))],
        out_specs=pl.BlockSpec((block_rows, d), lambda i: (i, 0)),
        compiler_params=_params(("parallel",), blocks, [((block_rows, d), F32)] * 2),
        name=name,
    )(x, g.reshape(1, d))


def _rope_table(pos, dim):
    inv = ROPE_THETA ** (-jnp.arange(0, dim, 2, dtype=F32) / dim)
    ang = pos.astype(F32)[:, None] * inv[None, :]
    ang = jnp.concatenate([ang, ang], axis=-1)
    return jnp.cos(ang), jnp.sin(ang)


def _axial_tables(seq, rot_dim):
    rows = seq // GRID_W
    row_idx = jnp.repeat(jnp.arange(rows, dtype=jnp.int32), GRID_W)
    col_idx = jnp.tile(jnp.arange(GRID_W, dtype=jnp.int32), rows)
    half = rot_dim // 2
    cos_r, sin_r = _rope_table(row_idx, half)
    cos_c, sin_c = _rope_table(col_idx, half)
    cos = jnp.concatenate([cos_r, cos_c], axis=-1)
    sin = jnp.concatenate([sin_r, sin_c], axis=-1)
    first = (jnp.arange(rot_dim) % half) < (half // 2)
    sin_up = jnp.where(first[None, :], -sin, 0.0)
    sin_dn = jnp.where(first[None, :], 0.0, sin)
    pad = ((0, 0), (0, LANES - rot_dim))
    return tuple(jnp.pad(t, pad) for t in (cos, sin_up, sin_dn))


def _rope_lanes(x, cos, sin_up, sin_dn, quarter):
    up = pltpu.roll(x, LANES - quarter, 1)
    dn = pltpu.roll(x, quarter, 1)
    return x * cos + up * sin_up + dn * sin_dn


def _dot_nt(a, bt):
    return lax.dot_general(a, bt.astype(BF16), (((1,), (1,)), ((), ())), preferred_element_type=F32)


def _in_proj_f32_kernel(h_ref, wt_ref, wtk_ref, z_ref, zk_ref):
    z_ref[...] = _dot_nt(h_ref[...], wt_ref[...])

    @pl.when(pl.program_id(1) == 0)
    def _():
        lane = lax.broadcasted_iota(jnp.int32, (1, LANES), 1)
        zk_ref[...] = jnp.where(lane < MLA_ROPE, _dot_nt(h_ref[...], wtk_ref[...]), 0.0)


def _in_proj_f32(h, w_in_t, n_cols, bm, name):
    t, k = h.shape
    latent_blocks = LATENT_COLS // IN_BLOCK
    skip = MLA_ROPE // SUBLANES

    def wt_index(i, j):
        tiles = j * (IN_BLOCK // SUBLANES) + jnp.where(j >= latent_blocks, skip, 0)
        return (tiles * SUBLANES, 0)

    pipelined = [((IN_BLOCK, k), F32), ((bm, IN_BLOCK), F32), ((bm, LANES), F32)]
    single = [((bm, k), BF16), ((LANES, k), F32), ((bm, IN_BLOCK), F32), ((IN_BLOCK, k), BF16)]
    return pl.pallas_call(
        _in_proj_f32_kernel,
        out_shape=(jax.ShapeDtypeStruct((t, n_cols), F32), jax.ShapeDtypeStruct((t, LANES), F32)),
        grid=(t // bm, n_cols // IN_BLOCK),
        in_specs=[pl.BlockSpec((bm, k), lambda i, j: (i, 0), pipeline_mode=_RESIDENT),
                  pl.BlockSpec((pl.Element(IN_BLOCK), pl.Element(k)), wt_index),
                  pl.BlockSpec((LANES, k), lambda i, j: (LATENT_COLS // LANES, 0), pipeline_mode=_RESIDENT)],
        out_specs=(pl.BlockSpec((bm, IN_BLOCK), lambda i, j: (i, j)),
                   pl.BlockSpec((bm, LANES), lambda i, j: (i, 0))),
        compiler_params=_params(("parallel", "arbitrary"), pipelined, single),
        name=name,
    )(h, w_in_t, w_in_t)


def _in_proj_sigmoid_kernel(h_ref, wt_ref, o_ref):
    o_ref[...] = _sigmoid(_dot_nt(h_ref[...], wt_ref[...])).astype(o_ref.dtype)


def _in_proj_rows(body, h, w_in_t, first_row, n_rows, out_dtype, bm, name):
    t, k = h.shape
    resident_h = _nbytes((bm, k), BF16) >= RESIDENT_MIN_BYTES
    pipelined = [((IN_BLOCK, k), F32), ((bm, IN_BLOCK), out_dtype)] + ([] if resident_h else [((bm, k), BF16)])
    single = [((bm, IN_BLOCK), F32), ((IN_BLOCK, k), BF16)] + ([((bm, k), BF16)] if resident_h else [])
    return pl.pallas_call(
        body,
        out_shape=jax.ShapeDtypeStruct((t, n_rows), out_dtype),
        grid=(t // bm, n_rows // IN_BLOCK),
        in_specs=[pl.BlockSpec((bm, k), lambda i, j: (i, 0), pipeline_mode=_RESIDENT if resident_h else None),
                  pl.BlockSpec((pl.Element(IN_BLOCK), pl.Element(k)),
                               lambda i, j: ((first_row // SUBLANES + j * (IN_BLOCK // SUBLANES)) * SUBLANES, 0))],
        out_specs=pl.BlockSpec((bm, IN_BLOCK), lambda i, j: (i, j)),
        compiler_params=_params(("parallel", "arbitrary"), pipelined, single),
        name=name,
    )(h, w_in_t)


def _lane_sum_mxu(x):
    ones = jnp.ones((LANES, LANES), BF16)
    hi = x.astype(BF16)
    lo = (x - hi.astype(F32)).astype(BF16)
    return jnp.dot(hi, ones, preferred_element_type=F32) + jnp.dot(lo, ones, preferred_element_type=F32)


def _gqa_prep_kernel(z_ref, g_ref, c_ref, cos_ref, su_ref, sd_ref, o_ref, *, heads_per_step, rope_blocks):
    @pl.when(pl.program_id(1) < rope_blocks)
    def _():
        cos, su, sd = cos_ref[...], su_ref[...], sd_ref[...]
        for h in range(heads_per_step):
            sl = slice(h * GQA_HEAD_DIM, (h + 1) * GQA_HEAD_DIM)
            x = z_ref[:, sl]
            ms = _lane_sum_mxu(x * x) * (1.0 / GQA_HEAD_DIM)
            y = x * lax.rsqrt(ms + EPS) * g_ref[:, sl]
            o_ref[:, sl] = (_rope_lanes(y, cos, su, sd, GQA_HEAD_DIM // 4) * c_ref[:, sl]).astype(o_ref.dtype)

    @pl.when(pl.program_id(1) >= rope_blocks)
    def _():
        o_ref[...] = z_ref[...].astype(o_ref.dtype)


def _gqa_prep(z, first_block, n_cols, gains, post_scale, tables, seq, br, heads_per_step, name):
    t = z.shape[0]
    bn = heads_per_step * GQA_HEAD_DIM
    rope_blocks = gains.shape[1] // bn
    s_blocks = seq // br
    tab_spec = pl.BlockSpec((br, LANES), lambda i, j: (i % s_blocks, 0))
    row_spec = pl.BlockSpec((1, bn), lambda i, j: (0, jnp.minimum(j, rope_blocks - 1)))
    blocks = [((br, bn), F32), ((br, bn), BF16)] + [((br, LANES), F32)] * 3
    return pl.pallas_call(
        functools.partial(_gqa_prep_kernel, heads_per_step=heads_per_step, rope_blocks=rope_blocks),
        out_shape=jax.ShapeDtypeStruct((t, n_cols), BF16),
        grid=(t // br, n_cols // bn),
        in_specs=[pl.BlockSpec((br, bn), lambda i, j: (i, first_block + j)), row_spec, row_spec,
                  tab_spec, tab_spec, tab_spec],
        out_specs=pl.BlockSpec((br, bn), lambda i, j: (i, j)),
        compiler_params=_params(("parallel", "arbitrary"), blocks, [((br, bn), F32)] * 2),
        name=name,
    )(z, gains, post_scale, *tables)


def _mla_q_kernel(z_ref, g_ref, w_ref, cos_ref, su_ref, sd_ref, o_ref, cq_ref, *, heads_per_step):
    @pl.when(pl.program_id(1) == 0)
    def _():
        z = z_ref[...]
        ms = jnp.mean(z * z, axis=-1, keepdims=True)
        cq_ref[...] = (z * lax.rsqrt(ms + EPS) * g_ref[...]).astype(cq_ref.dtype)

    w = w_ref[...]
    n_chains = o_ref.shape[0] // ROW_CHAIN
    for r in range(n_chains):
        rows = slice(r * ROW_CHAIN, (r + 1) * ROW_CHAIN)
        q = _dot(cq_ref[rows, :], w) * (MLA_SCALE * LOG2_E)
        cos, su, sd = cos_ref[rows, :], su_ref[rows, :], sd_ref[rows, :]
        for h in range(heads_per_step):
            base = h * MLA_QK_PAD
            o_ref[rows, base:base + MLA_NOPE] = q[:, base:base + MLA_NOPE].astype(o_ref.dtype)
            pe = q[:, base + MLA_NOPE:base + MLA_QK_PAD]
            o_ref[rows, base + MLA_NOPE:base + MLA_QK_PAD] = _rope_lanes(
                pe, cos, su, sd, MLA_ROPE // 4).astype(o_ref.dtype)


def _mla_q_proj(z_lat, g_q_a, w_q, tables, seq, bm, heads_per_step, name):
    t = z_lat.shape[0]
    bn = heads_per_step * MLA_QK_PAD
    n = w_q.shape[1]
    s_blocks = seq // bm
    tab_spec = pl.BlockSpec((bm, LANES), lambda i, j: (i % s_blocks, 0))
    blocks = [((bm, Q_LORA), F32), ((Q_LORA, bn), BF16), ((bm, bn), BF16)] + [((bm, LANES), F32)] * 3
    return pl.pallas_call(
        functools.partial(_mla_q_kernel, heads_per_step=heads_per_step),
        out_shape=jax.ShapeDtypeStruct((t, n), BF16),
        grid=(t // bm, n // bn),
        in_specs=[pl.BlockSpec((bm, Q_LORA), lambda i, j: (i, 0)),
                  pl.BlockSpec((1, Q_LORA), lambda i, j: (0, 0)),
                  pl.BlockSpec((Q_LORA, bn), lambda i, j: (0, j)),
                  tab_spec, tab_spec, tab_spec],
        out_specs=pl.BlockSpec((bm, bn), lambda i, j: (i, j)),
        scratch_shapes=[pltpu.VMEM((bm, Q_LORA), BF16)],
        compiler_params=_params(("parallel", "arbitrary"), blocks,
                                [((bm, Q_LORA), BF16), ((bm, bn), F32), ((bm, Q_LORA), F32)]),
        name=name,
    )(z_lat, g_q_a.reshape(1, Q_LORA), w_q, *tables)


def _mla_kv_kernel(z_ref, pe_ref, g_ref, w_ref, cos_ref, su_ref, sd_ref, k_ref, v_ref, ckv_ref, kpe_ref,
                   *, heads_per_step):
    @pl.when(pl.program_id(1) == 0)
    def _():
        z = z_ref[...]
        ms = jnp.mean(z * z, axis=-1, keepdims=True)
        ckv_ref[...] = (z * lax.rsqrt(ms + EPS) * g_ref[...]).astype(ckv_ref.dtype)
        kpe_ref[...] = _rope_lanes(pe_ref[...], cos_ref[...], su_ref[...], sd_ref[...],
                                   MLA_ROPE // 4).astype(kpe_ref.dtype)

    kv = _dot(ckv_ref[...], w_ref[...])
    for h in range(heads_per_step):
        src = h * (MLA_NOPE + MLA_V)
        dst = h * MLA_QK_PAD
        k_ref[:, dst:dst + MLA_NOPE] = kv[:, src:src + MLA_NOPE].astype(k_ref.dtype)
        k_ref[:, dst + MLA_NOPE:dst + MLA_QK_PAD] = kpe_ref[...]
        v_ref[:, h * MLA_V:(h + 1) * MLA_V] = kv[:, src + MLA_NOPE:src + MLA_NOPE + MLA_V].astype(v_ref.dtype)


def _mla_kv_proj(z_lat, z_kpe, g_kv_a, w_kv, tables, seq, bm, heads_per_step, name):
    t = z_lat.shape[0]
    bn = heads_per_step * (MLA_NOPE + MLA_V)
    steps = w_kv.shape[1] // bn
    s_blocks = seq // bm
    tab_spec = pl.BlockSpec((bm, LANES), lambda i, j: (i % s_blocks, 0))
    kva_block = Q_LORA // KV_LORA
    bk, bv = heads_per_step * MLA_QK_PAD, heads_per_step * MLA_V
    blocks = ([((bm, KV_LORA), F32), ((bm, LANES), F32), ((KV_LORA, bn), BF16), ((bm, bk), BF16),
               ((bm, bv), BF16)] + [((bm, LANES), F32)] * 3)
    return pl.pallas_call(
        functools.partial(_mla_kv_kernel, heads_per_step=heads_per_step),
        out_shape=(jax.ShapeDtypeStruct((t, steps * bk), BF16), jax.ShapeDtypeStruct((t, steps * bv), BF16)),
        grid=(t // bm, steps),
        in_specs=[pl.BlockSpec((bm, KV_LORA), lambda i, j: (i, kva_block)),
                  pl.BlockSpec((bm, LANES), lambda i, j: (i, 0)),
                  pl.BlockSpec((1, KV_LORA), lambda i, j: (0, 0)),
                  pl.BlockSpec((KV_LORA, bn), lambda i, j: (0, j)),
                  tab_spec, tab_spec, tab_spec],
        out_specs=(pl.BlockSpec((bm, bk), lambda i, j: (i, j)),
                   pl.BlockSpec((bm, bv), lambda i, j: (i, j))),
        scratch_shapes=[pltpu.VMEM((bm, KV_LORA), BF16), pltpu.VMEM((bm, LANES), BF16)],
        compiler_params=_params(("parallel", "arbitrary"), blocks,
                                [((bm, KV_LORA), BF16), ((bm, bn), F32), ((bm, KV_LORA), F32)]),
        name=name,
    )(z_lat, z_kpe, g_kv_a.reshape(1, KV_LORA), w_kv, *tables)


def _attention_kernel(q_ref, k_ref, v_ref, *refs, bq, q_heads, group, dk, dv, n_side):
    side_in, o_ref, side_out = refs[:n_side], refs[n_side], refs[n_side + 1:]
    for w_ref, w16_ref in zip(side_in, side_out):
        w16_ref[...] = w_ref[...].astype(w16_ref.dtype)
    for h in range(q_heads):
        g = h // group
        k = k_ref[:, g * dk:(g + 1) * dk]
        v = v_ref[:, g * dv:(g + 1) * dv]
        for qi in range(q_ref.shape[0] // bq):
            rows = slice(qi * bq, (qi + 1) * bq)
            s = lax.dot_general(q_ref[rows, h * dk:(h + 1) * dk], k, (((1,), (1,)), ((), ())),
                                preferred_element_type=F32)
            m = jnp.max(s, axis=-1, keepdims=True)
            p = jnp.exp2(s - m)
            l = jnp.sum(p, axis=-1, keepdims=True)
            o = jnp.dot(p.astype(BF16), v, preferred_element_type=F32)
            o_ref[rows, h * dv:(h + 1) * dv] = (o / l).astype(o_ref.dtype)


def _attention(q, k, v, *, n_q_heads, group, q_heads_per_step, dk, dv, k_head_offset, v_head_offset, bq,
               name, side_weights=()):
    b, s, _ = q.shape
    nq = q_heads_per_step
    nk = max(1, nq // group)
    assert (nq % group == 0 or group % nq == 0) and k_head_offset % nk == 0 and v_head_offset % nk == 0
    n_h = n_q_heads // nq
    chains = nq * (s // bq)
    blocks = [((s, nq * dk), BF16), ((s, nk * dk), BF16), ((s, nk * dv), BF16), ((s, nq * dv), BF16)]
    side_specs, side_shapes = [], []
    for w in side_weights:
        chunk = w.shape[0] // (b * n_h)
        assert chunk * b * n_h == w.shape[0]
        side_specs.append(pl.BlockSpec((chunk, w.shape[1]), lambda bi, h: (bi * n_h + h, 0)))
        side_shapes.append(jax.ShapeDtypeStruct(w.shape, BF16))
        blocks += [((chunk, w.shape[1]), F32), ((chunk, w.shape[1]), BF16)]
    out = pl.pallas_call(
        functools.partial(_attention_kernel, bq=bq, q_heads=nq, group=group, dk=dk, dv=dv,
                          n_side=len(side_weights)),
        out_shape=[jax.ShapeDtypeStruct((b, s, n_q_heads * dv), BF16)] + side_shapes,
        grid=(b, n_h),
        in_specs=[pl.BlockSpec((None, s, nq * dk), lambda bi, h: (bi, 0, h)),
                  pl.BlockSpec((None, s, nk * dk), lambda bi, h: (bi, 0, (k_head_offset + h * nq // group) // nk)),
                  pl.BlockSpec((None, s, nk * dv), lambda bi, h: (bi, 0, (v_head_offset + h * nq // group) // nk))]
                 + side_specs,
        out_specs=[pl.BlockSpec((None, s, nq * dv), lambda bi, h: (bi, 0, h))] + side_specs,
        compiler_params=_params(("parallel", "arbitrary"), blocks,
                                [((bq, s), F32)] * chains + [((bq, s), BF16)] * chains),
        name=name,
    )(q, k, v, *side_weights)
    return out[0], out[1:]


def _merge_kernel(oa_ref, ob_ref, wa_ref, wb_ref, ga_ref, gb_ref, o_ref):
    pa = _dot(oa_ref[...], wa_ref[...])
    pb = _dot(ob_ref[...], wb_ref[...])
    o_ref[...] = (ga_ref[...].astype(F32) * pa + gb_ref[...].astype(F32) * pb).astype(o_ref.dtype)


def _gated_merge(o_a, o_b, w_a, w_b, gates, gate_block_offset, bm, bn, name):
    t, ka = o_a.shape
    kb = o_b.shape[1]
    n = w_a.shape[1]
    nb = n // bn
    pipelined = [((ka, bn), w_a.dtype), ((kb, bn), w_b.dtype), ((bm, bn), BF16), ((bm, bn), BF16),
                 ((bm, bn), BF16)]
    single = [((bm, ka), BF16), ((bm, kb), BF16)] + [((bm, bn), F32)] * 2
    return pl.pallas_call(
        _merge_kernel,
        out_shape=jax.ShapeDtypeStruct((t, n), BF16),
        grid=(t // bm, nb),
        in_specs=[pl.BlockSpec((bm, ka), lambda i, j: (i, 0), pipeline_mode=_RESIDENT),
                  pl.BlockSpec((bm, kb), lambda i, j: (i, 0), pipeline_mode=_RESIDENT),
                  pl.BlockSpec((ka, bn), lambda i, j: (0, j)),
                  pl.BlockSpec((kb, bn), lambda i, j: (0, j)),
                  pl.BlockSpec((bm, bn), lambda i, j: (i, gate_block_offset + j)),
                  pl.BlockSpec((bm, bn), lambda i, j: (i, gate_block_offset + nb + j))],
        out_specs=pl.BlockSpec((bm, bn), lambda i, j: (i, j)),
        compiler_params=_params(("parallel", "arbitrary"), pipelined, single),
        name=name,
    )(o_a, o_b, w_a, w_b, gates, gates)


def _mm_residual_kernel(a_ref, b_ref, r_ref, o_ref):
    o_ref[...] = r_ref[...] + _dot(a_ref[...], b_ref[...])


def _matmul_residual(a, b, r, bm, bn, name, k_block_index, k_block):
    m = a.shape[0]
    k = k_block
    n = b.shape[1]
    pipelined = [((bm, k), a.dtype), ((k, bn), b.dtype), ((bm, bn), F32), ((bm, bn), F32)]
    single = [((bm, bn), F32)]
    return pl.pallas_call(
        _mm_residual_kernel,
        out_shape=jax.ShapeDtypeStruct((m, n), F32),
        grid=(m // bm, n // bn),
        in_specs=[pl.BlockSpec((bm, k), lambda i, j: (i, k_block_index)),
                  pl.BlockSpec((k, bn), lambda i, j: (k_block_index, j)),
                  pl.BlockSpec((bm, bn), lambda i, j: (i, j))],
        out_specs=pl.BlockSpec((bm, bn), lambda i, j: (i, j)),
        compiler_params=_params(("parallel", "arbitrary"), pipelined, single),
        name=name,
    )(a, b, r)


def _swiglu_kernel(a_ref, wg_ref, wu_ref, wd_ref, o_ref, wd16_ref):
    a = a_ref[...]
    g = _dot(a, wg_ref[...])
    u = _dot(a, wu_ref[...])
    o_ref[...] = (g * _sigmoid(g) * u).astype(o_ref.dtype)

    @pl.when(pl.program_id(0) == 0)
    def _():
        wd16_ref[...] = wd_ref[...].astype(wd16_ref.dtype)


def _swiglu(a, w_gate, w_up, w_down, bm, bn, name):
    t, k = a.shape
    n = w_gate.shape[1]
    nj = n // bn
    d_out = w_down.shape[1]
    chunk = w_down.shape[0] // nj

    def wd_index(i, j):
        return (jnp.where(i == 0, j, nj - 1), 0)

    pipelined = [((k, bn), w_gate.dtype), ((k, bn), w_up.dtype), ((bm, bn), BF16),
                 ((chunk, d_out), w_down.dtype), ((chunk, d_out), BF16)]
    single = [((bm, k), BF16)] + [((bm, bn), F32)] * 3
    return pl.pallas_call(
        _swiglu_kernel,
        out_shape=(jax.ShapeDtypeStruct((t, n), BF16), jax.ShapeDtypeStruct(w_down.shape, BF16)),
        grid=(t // bm, nj),
        in_specs=[pl.BlockSpec((bm, k), lambda i, j: (i, 0), pipeline_mode=_RESIDENT),
                  pl.BlockSpec((k, bn), lambda i, j: (0, j)),
                  pl.BlockSpec((k, bn), lambda i, j: (0, j)),
                  pl.BlockSpec((chunk, d_out), wd_index)],
        out_specs=(pl.BlockSpec((bm, bn), lambda i, j: (i, j)),
                   pl.BlockSpec((chunk, d_out), wd_index)),
        compiler_params=_params(("arbitrary", "arbitrary"), pipelined, single),
        name=name,
    )(a, w_gate, w_up, w_down)


def _layer(xf, batch, seq, g_attn, w_in, g_q_a, w_q_b, g_kv_a, w_kv_b, g_qn, g_kn,
           w_branch_a, w_branch_b, w_o, g_ffn, w_gate, w_up, w_down):
    t, d = xf.shape
    w_q = jnp.pad(w_q_b.reshape(Q_LORA, MLA_HEADS, MLA_NOPE + MLA_ROPE),
                  ((0, 0), (0, 0), (0, MLA_QK_PAD - MLA_NOPE - MLA_ROPE)))
    w_q = w_q.reshape(Q_LORA, MLA_HEADS * MLA_QK_PAD).astype(BF16)
    w_kv = w_kv_b.astype(BF16)
    gqa_gains = jnp.concatenate([jnp.tile(g_qn, GQA_Q_HEADS), jnp.tile(g_kn, GQA_KV_HEADS)]).reshape(1, -1)
    gqa_post = jnp.concatenate([jnp.full((GQA_Q_HEADS * GQA_HEAD_DIM,), GQA_SCALE * LOG2_E, F32),
                                jnp.ones((GQA_KV_HEADS * GQA_HEAD_DIM,), F32)]).reshape(1, -1)
    mla_tables = _axial_tables(seq, MLA_ROPE)
    gqa_tables = _axial_tables(seq, GQA_HEAD_DIM)

    h = _rmsnorm(xf, g_attn, BF16, "norm_attn")
    w_in_t = w_in.T
    n_qkv = (GQA_Q_HEADS + 2 * GQA_KV_HEADS) * GQA_HEAD_DIM
    z, z_kpe = _in_proj_f32(h, w_in_t, LATENT_COLS + n_qkv, 2048, "in_proj_f32")
    gates = _in_proj_rows(_in_proj_sigmoid_kernel, h, w_in_t, LATENT_COLS + MLA_ROPE + n_qkv, 2 * d, BF16,
                          2048, "in_proj_gate")
    qkv_b = _gqa_prep(z, LATENT_COLS // IN_BLOCK, n_qkv, gqa_gains, gqa_post, gqa_tables, seq, 1024,
                      IN_BLOCK // GQA_HEAD_DIM, "gqa_prep")

    q_a = _mla_q_proj(z, g_q_a, w_q, mla_tables, seq, 1024, 4, "mla_q_proj")
    k_a, v_a = _mla_kv_proj(z, z_kpe, g_kv_a, w_kv, mla_tables, seq, 1024, 8, "mla_kv_proj")
    o_a, (w_o16,) = _attention(
        q_a.reshape(batch, seq, -1), k_a.reshape(batch, seq, -1), v_a.reshape(batch, seq, -1),
        n_q_heads=MLA_HEADS, group=1, q_heads_per_step=2, dk=MLA_QK_PAD, dv=MLA_V,
        k_head_offset=0, v_head_offset=0, bq=512, name="mla_attention", side_weights=(w_o,))

    qkv_b3 = qkv_b.reshape(batch, seq, -1)
    o_b, (w_a16, w_b16) = _attention(
        qkv_b3, qkv_b3, qkv_b3, n_q_heads=GQA_Q_HEADS,
        group=GQA_Q_HEADS // GQA_KV_HEADS, q_heads_per_step=2, dk=GQA_HEAD_DIM, dv=GQA_HEAD_DIM,
        k_head_offset=GQA_Q_HEADS, v_head_offset=GQA_Q_HEADS + GQA_KV_HEADS, bq=512, name="gqa_attention",
        side_weights=(w_branch_a, w_branch_b))

    m = _gated_merge(o_a.reshape(t, -1), o_b.reshape(t, -1), w_a16, w_b16, gates, 0, 2048, 512,
                     "gated_merge")
    x1 = _matmul_residual(m, w_o16, xf, 1024, 1024, "out_proj", k_block_index=0, k_block=d)

    h2 = _rmsnorm(x1, g_ffn, BF16, "norm_ffn")
    act, wd = _swiglu(h2, w_gate, w_up, w_down, 2048, 256, "ffn_gate_up")
    k_half = act.shape[1] // 2
    y = _matmul_residual(act, wd, x1, 1024, 512, "ffn_down_lo", k_block_index=0, k_block=k_half)
    return _matmul_residual(act, wd, y, 1024, 512, "ffn_down_hi", k_block_index=1, k_block=k_half)


def kernel(x, g_attn, w_in, g_q_a, w_q_b, g_kv_a, w_kv_b, g_qn, g_kn, w_branch_a, w_branch_b, w_o,
           g_ffn, w_gate, w_up, w_down, g_final):
    batch, seq, d = x.shape
    xf = x.reshape(batch * seq, d)
    for l in range(g_attn.shape[0]):
        xf = _layer(xf, batch, seq, g_attn[l], w_in[l], g_q_a[l], w_q_b[l], g_kv_a[l], w_kv_b[l],
                    g_qn[l], g_kn[l], w_branch_a[l], w_branch_b[l], w_o[l], g_ffn[l],
                    w_gate[l], w_up[l], w_down[l])
    return _rmsnorm(xf, g_final, F32, "norm_final").reshape(batch, seq, d)
```

```python
import functools

import jax
import jax.numpy as jnp
import numpy as np
from jax import lax
from jax.experimental import pallas as pl
from jax.experimental.pallas import tpu as pltpu

GRID_W = 64
ROPE_THETA = 10000.0
EPS = 1e-6

MLA_HEADS = 16
MLA_NOPE = 128
MLA_ROPE = 64
MLA_V = 128
MLA_QK_PAD = 256
Q_LORA = 1024
KV_LORA = 512
MLA_SCALE = (MLA_NOPE + MLA_ROPE) ** -0.5

GQA_Q_HEADS = 16
GQA_KV_HEADS = 4
GQA_HEAD_DIM = 128
GQA_SCALE = GQA_HEAD_DIM ** -0.5
LOG2_E = 1.4426950408889634

LANES = 128
SUBLANES = 8
V7X_VMEM_BYTES = 64 * 1024 * 1024
VMEM_CAP_BYTES = V7X_VMEM_BYTES - 6 * 1024 * 1024

IN_BLOCK = 512
TAIL_SPLIT = 2
ROW_CHAIN = 256
LATENT_COLS = Q_LORA + KV_LORA

BF16 = jnp.bfloat16
F32 = jnp.float32

_RESIDENT = pl.Buffered(1)
RESIDENT_MIN_BYTES = 16 * 1024 * 1024


def _nbytes(shape, dtype):
    return int(np.prod(shape)) * jnp.dtype(dtype).itemsize


def _params(semantics, pipelined_blocks, single_blocks=()):
    est = 2 * sum(_nbytes(s, d) for s, d in pipelined_blocks)
    est += sum(_nbytes(s, d) for s, d in single_blocks)
    limit = min(VMEM_CAP_BYTES, int(est * 1.25) + (4 << 20))
    return pltpu.CompilerParams(dimension_semantics=semantics, vmem_limit_bytes=limit)


def _dot(a, b):
    return jnp.dot(a, b.astype(BF16), preferred_element_type=F32)


def _sigmoid(x):
    return 0.5 * jnp.tanh(0.5 * x) + 0.5


def _rmsnorm_kernel(x_ref, g_ref, o_ref):
    x = x_ref[...]
    ms = jnp.mean(x * x, axis=-1, keepdims=True)
    o_ref[...] = (x * lax.rsqrt(ms + EPS) * g_ref[...]).astype(o_ref.dtype)


def _rmsnorm(x, g, out_dtype, name, block_rows=256):
    t, d = x.shape
    blocks = [((block_rows, d), F32), ((block_rows, d), out_dtype)]
    return pl.pallas_call(
        _rmsnorm_kernel,
        out_shape=jax.ShapeDtypeStruct((t, d), out_dtype),
        grid=(t // block_rows,),
        in_specs=[pl.BlockSpec((block_rows, d), lambda i: (i, 0)),
                  pl.BlockSpec((1, d), lambda i: (0, 0))],
        out_specs=pl.BlockSpec((block_rows, d), lambda i: (i, 0)),
        compiler_params=_params(("parallel",), blocks, [((block_rows, d), F32)] * 2),
        name=name,
    )(x, g.reshape(1, d))


def _rope_table(pos, dim):
    inv = ROPE_THETA ** (-jnp.arange(0, dim, 2, dtype=F32) / dim)
    ang = pos.astype(F32)[:, None] * inv[None, :]
    ang = jnp.concatenate([ang, ang], axis=-1)
    return jnp.cos(ang), jnp.sin(ang)


def _axial_tables(seq, rot_dim):
    rows = seq // GRID_W
    row_idx = jnp.repeat(jnp.arange(rows, dtype=jnp.int32), GRID_W)
    col_idx = jnp.tile(jnp.arange(GRID_W, dtype=jnp.int32), rows)
    half = rot_dim // 2
    cos_r, sin_r = _rope_table(row_idx, half)
    cos_c, sin_c = _rope_table(col_idx, half)
    cos = jnp.concatenate([cos_r, cos_c], axis=-1)
    sin = jnp.concatenate([sin_r, sin_c], axis=-1)
    first = (jnp.arange(rot_dim) % half) < (half // 2)
    sin_up = jnp.where(first[None, :], -sin, 0.0)
    sin_dn = jnp.where(first[None, :], 0.0, sin)
    pad = ((0, 0), (0, LANES - rot_dim))
    return tuple(jnp.pad(t, pad) for t in (cos, sin_up, sin_dn))


def _rope_lanes(x, cos, sin_up, sin_dn, quarter):
    up = pltpu.roll(x, LANES - quarter, 1)
    dn = pltpu.roll(x, quarter, 1)
    return x * cos + up * sin_up + dn * sin_dn


def _dot_nt(a, bt):
    return lax.dot_general(a, bt.astype(BF16), (((1,), (1,)), ((), ())), preferred_element_type=F32)


def _in_proj_f32_kernel(h_ref, wt_ref, wtk_ref, z_ref, zk_ref):
    z_ref[...] = _dot_nt(h_ref[...], wt_ref[...])

    @pl.when(pl.program_id(1) == 0)
    def _():
        lane = lax.broadcasted_iota(jnp.int32, (1, LANES), 1)
        zk_ref[...] = jnp.where(lane < MLA_ROPE, _dot_nt(h_ref[...], wtk_ref[...]), 0.0)


def _in_proj_f32(h, w_in_t, n_cols, bm, name):
    t, k = h.shape
    latent_blocks = LATENT_COLS // IN_BLOCK
    skip = MLA_ROPE // SUBLANES

    def wt_index(i, j):
        tiles = j * (IN_BLOCK // SUBLANES) + jnp.where(j >= latent_blocks, skip, 0)
        return (tiles * SUBLANES, 0)

    pipelined = [((IN_BLOCK, k), F32), ((bm, IN_BLOCK), F32), ((bm, LANES), F32)]
    single = [((bm, k), BF16), ((LANES, k), F32), ((bm, IN_BLOCK), F32), ((IN_BLOCK, k), BF16)]
    return pl.pallas_call(
        _in_proj_f32_kernel,
        out_shape=(jax.ShapeDtypeStruct((t, n_cols), F32), jax.ShapeDtypeStruct((t, LANES), F32)),
        grid=(t // bm, n_cols // IN_BLOCK),
        in_specs=[pl.BlockSpec((bm, k), lambda i, j: (i, 0), pipeline_mode=_RESIDENT),
                  pl.BlockSpec((pl.Element(IN_BLOCK), pl.Element(k)), wt_index),
                  pl.BlockSpec((LANES, k), lambda i, j: (LATENT_COLS // LANES, 0), pipeline_mode=_RESIDENT)],
        out_specs=(pl.BlockSpec((bm, IN_BLOCK), lambda i, j: (i, j)),
                   pl.BlockSpec((bm, LANES), lambda i, j: (i, 0))),
        compiler_params=_params(("parallel", "arbitrary"), pipelined, single),
        name=name,
    )(h, w_in_t, w_in_t)


def _in_proj_sigmoid_kernel(h_ref, wt_ref, o_ref):
    o_ref[...] = _sigmoid(_dot_nt(h_ref[...], wt_ref[...])).astype(o_ref.dtype)


def _in_proj_rows(body, h, w_in_t, first_row, n_rows, out_dtype, bm, name):
    t, k = h.shape
    resident_h = _nbytes((bm, k), BF16) >= RESIDENT_MIN_BYTES
    pipelined = [((IN_BLOCK, k), F32), ((bm, IN_BLOCK), out_dtype)] + ([] if resident_h else [((bm, k), BF16)])
    single = [((bm, IN_BLOCK), F32), ((IN_BLOCK, k), BF16)] + ([((bm, k), BF16)] if resident_h else [])
    return pl.pallas_call(
        body,
        out_shape=jax.ShapeDtypeStruct((t, n_rows), out_dtype),
        grid=(t // bm, n_rows // IN_BLOCK),
        in_specs=[pl.BlockSpec((bm, k), lambda i, j: (i, 0), pipeline_mode=_RESIDENT if resident_h else None),
                  pl.BlockSpec((pl.Element(IN_BLOCK), pl.Element(k)),
                               lambda i, j: ((first_row // SUBLANES + j * (IN_BLOCK // SUBLANES)) * SUBLANES, 0))],
        out_specs=pl.BlockSpec((bm, IN_BLOCK), lambda i, j: (i, j)),
        compiler_params=_params(("parallel", "arbitrary"), pipelined, single),
        name=name,
    )(h, w_in_t)


def _lane_sum_mxu(x):
    ones = jnp.ones((LANES, LANES), BF16)
    hi = x.astype(BF16)
    lo = (x - hi.astype(F32)).astype(BF16)
    return jnp.dot(hi, ones, preferred_element_type=F32) + jnp.dot(lo, ones, preferred_element_type=F32)


def _gqa_prep_kernel(z_ref, g_ref, c_ref, cos_ref, su_ref, sd_ref, o_ref, *, heads_per_step, rope_blocks):
    @pl.when(pl.program_id(1) < rope_blocks)
    def _():
        cos, su, sd = cos_ref[...], su_ref[...], sd_ref[...]
        for h in range(heads_per_step):
            sl = slice(h * GQA_HEAD_DIM, (h + 1) * GQA_HEAD_DIM)
            x = z_ref[:, sl]
            ms = _lane_sum_mxu(x * x) * (1.0 / GQA_HEAD_DIM)
            y = x * lax.rsqrt(ms + EPS) * g_ref[:, sl]
            o_ref[:, sl] = (_rope_lanes(y, cos, su, sd, GQA_HEAD_DIM // 4) * c_ref[:, sl]).astype(o_ref.dtype)

    @pl.when(pl.program_id(1) >= rope_blocks)
    def _():
        o_ref[...] = z_ref[...].astype(o_ref.dtype)


def _gqa_prep(z, first_block, n_cols, gains, post_scale, tables, seq, br, heads_per_step, name):
    t = z.shape[0]
    bn = heads_per_step * GQA_HEAD_DIM
    rope_blocks = gains.shape[1] // bn
    s_blocks = seq // br
    tab_spec = pl.BlockSpec((br, LANES), lambda i, j: (i % s_blocks, 0))
    row_spec = pl.BlockSpec((1, bn), lambda i, j: (0, jnp.minimum(j, rope_blocks - 1)))
    blocks = [((br, bn), F32), ((br, bn), BF16)] + [((br, LANES), F32)] * 3
    return pl.pallas_call(
        functools.partial(_gqa_prep_kernel, heads_per_step=heads_per_step, rope_blocks=rope_blocks),
        out_shape=jax.ShapeDtypeStruct((t, n_cols), BF16),
        grid=(t // br, n_cols // bn),
        in_specs=[pl.BlockSpec((br, bn), lambda i, j: (i, first_block + j)), row_spec, row_spec,
                  tab_spec, tab_spec, tab_spec],
        out_specs=pl.BlockSpec((br, bn), lambda i, j: (i, j)),
        compiler_params=_params(("parallel", "arbitrary"), blocks, [((br, bn), F32)] * 2),
        name=name,
    )(z, gains, post_scale, *tables)


def _mla_q_kernel(z_ref, g_ref, w_ref, cos_ref, su_ref, sd_ref, o_ref, cq_ref, *, heads_per_step):
    @pl.when(pl.program_id(1) == 0)
    def _():
        z = z_ref[...]
        ms = jnp.mean(z * z, axis=-1, keepdims=True)
        cq_ref[...] = (z * lax.rsqrt(ms + EPS) * g_ref[...]).astype(cq_ref.dtype)

    w = w_ref[...]
    n_chains = o_ref.shape[0] // ROW_CHAIN
    for r in range(n_chains):
        rows = slice(r * ROW_CHAIN, (r + 1) * ROW_CHAIN)
        q = _dot(cq_ref[rows, :], w) * (MLA_SCALE * LOG2_E)
        cos, su, sd = cos_ref[rows, :], su_ref[rows, :], sd_ref[rows, :]
        for h in range(heads_per_step):
            base = h * MLA_QK_PAD
            o_ref[rows, base:base + MLA_NOPE] = q[:, base:base + MLA_NOPE].astype(o_ref.dtype)
            pe = q[:, base + MLA_NOPE:base + MLA_QK_PAD]
            o_ref[rows, base + MLA_NOPE:base + MLA_QK_PAD] = _rope_lanes(
                pe, cos, su, sd, MLA_ROPE // 4).astype(o_ref.dtype)


def _mla_q_proj(z_lat, g_q_a, w_q, tables, seq, bm, heads_per_step, name):
    t = z_lat.shape[0]
    bn = heads_per_step * MLA_QK_PAD
    n = w_q.shape[1]
    s_blocks = seq // bm
    tab_spec = pl.BlockSpec((bm, LANES), lambda i, j: (i % s_blocks, 0))
    blocks = [((bm, Q_LORA), F32), ((Q_LORA, bn), BF16), ((bm, bn), BF16)] + [((bm, LANES), F32)] * 3
    return pl.pallas_call(
        functools.partial(_mla_q_kernel, heads_per_step=heads_per_step),
        out_shape=jax.ShapeDtypeStruct((t, n), BF16),
        grid=(t // bm, n // bn),
        in_specs=[pl.BlockSpec((bm, Q_LORA), lambda i, j: (i, 0)),
                  pl.BlockSpec((1, Q_LORA), lambda i, j: (0, 0)),
                  pl.BlockSpec((Q_LORA, bn), lambda i, j: (0, j)),
                  tab_spec, tab_spec, tab_spec],
        out_specs=pl.BlockSpec((bm, bn), lambda i, j: (i, j)),
        scratch_shapes=[pltpu.VMEM((bm, Q_LORA), BF16)],
        compiler_params=_params(("parallel", "arbitrary"), blocks,
                                [((bm, Q_LORA), BF16), ((bm, bn), F32), ((bm, Q_LORA), F32)]),
        name=name,
    )(z_lat, g_q_a.reshape(1, Q_LORA), w_q, *tables)


def _mla_kv_kernel(z_ref, pe_ref, g_ref, w_ref, cos_ref, su_ref, sd_ref, k_ref, v_ref, ckv_ref, kpe_ref,
                   *, heads_per_step):
    @pl.when(pl.program_id(1) == 0)
    def _():
        z = z_ref[...]
        ms = jnp.mean(z * z, axis=-1, keepdims=True)
        ckv_ref[...] = (z * lax.rsqrt(ms + EPS) * g_ref[...]).astype(ckv_ref.dtype)
        kpe_ref[...] = _rope_lanes(pe_ref[...], cos_ref[...], su_ref[...], sd_ref[...],
                                   MLA_ROPE // 4).astype(kpe_ref.dtype)

    kv = _dot(ckv_ref[...], w_ref[...])
    for h in range(heads_per_step):
        src = h * (MLA_NOPE + MLA_V)
        dst = h * MLA_QK_PAD
        k_ref[:, dst:dst + MLA_NOPE] = kv[:, src:src + MLA_NOPE].astype(k_ref.dtype)
        k_ref[:, dst + MLA_NOPE:dst + MLA_QK_PAD] = kpe_ref[...]
        v_ref[:, h * MLA_V:(h + 1) * MLA_V] = kv[:, src + MLA_NOPE:src + MLA_NOPE + MLA_V].astype(v_ref.dtype)


def _mla_kv_proj(z_lat, z_kpe, g_kv_a, w_kv, tables, seq, bm, heads_per_step, name):
    t = z_lat.shape[0]
    bn = heads_per_step * (MLA_NOPE + MLA_V)
    steps = w_kv.shape[1] // bn
    s_blocks = seq // bm
    tab_spec = pl.BlockSpec((bm, LANES), lambda i, j: (i % s_blocks, 0))
    kva_block = Q_LORA // KV_LORA
    bk, bv = heads_per_step * MLA_QK_PAD, heads_per_step * MLA_V
    blocks = ([((bm, KV_LORA), F32), ((bm, LANES), F32), ((KV_LORA, bn), w_kv.dtype), ((bm, bk), BF16),
               ((bm, bv), BF16)] + [((bm, LANES), F32)] * 3)
    return pl.pallas_call(
        functools.partial(_mla_kv_kernel, heads_per_step=heads_per_step),
        out_shape=(jax.ShapeDtypeStruct((t, steps * bk), BF16), jax.ShapeDtypeStruct((t, steps * bv), BF16)),
        grid=(t // bm, steps),
        in_specs=[pl.BlockSpec((bm, KV_LORA), lambda i, j: (i, kva_block)),
                  pl.BlockSpec((bm, LANES), lambda i, j: (i, 0)),
                  pl.BlockSpec((1, KV_LORA), lambda i, j: (0, 0)),
                  pl.BlockSpec((KV_LORA, bn), lambda i, j: (0, j)),
                  tab_spec, tab_spec, tab_spec],
        out_specs=(pl.BlockSpec((bm, bk), lambda i, j: (i, j)),
                   pl.BlockSpec((bm, bv), lambda i, j: (i, j))),
        scratch_shapes=[pltpu.VMEM((bm, KV_LORA), BF16), pltpu.VMEM((bm, LANES), BF16)],
        compiler_params=_params(("parallel", "arbitrary"), blocks,
                                [((bm, KV_LORA), BF16), ((bm, bn), F32), ((bm, KV_LORA), F32)]),
        name=name,
    )(z_lat, z_kpe, g_kv_a.reshape(1, KV_LORA), w_kv, *tables)


def _attention_kernel(q_ref, k_ref, v_ref, *refs, bq, q_heads, group, dk, dv, n_side):
    side_in, o_ref, side_out = refs[:n_side], refs[n_side], refs[n_side + 1:]
    for w_ref, w16_ref in zip(side_in, side_out):
        w16_ref[...] = w_ref[...].astype(w16_ref.dtype)
    chains = [(h, r0, bq) for h in range(q_heads) for r0 in range(0, q_ref.shape[0], bq)]
    h_last, r_last, _ = chains.pop()
    part = bq // TAIL_SPLIT
    chains += [(h_last, r_last + c * part, part) for c in range(TAIL_SPLIT)]
    for h, r0, n in chains:
        g = h // group
        k = k_ref[:, g * dk:(g + 1) * dk]
        v = v_ref[:, g * dv:(g + 1) * dv]
        rows = slice(r0, r0 + n)
        s = lax.dot_general(q_ref[rows, h * dk:(h + 1) * dk], k, (((1,), (1,)), ((), ())),
                            preferred_element_type=F32)
        m = jnp.max(s, axis=-1, keepdims=True)
        p = jnp.exp2(s - m)
        l = jnp.sum(p, axis=-1, keepdims=True)
        o = jnp.dot(p.astype(BF16), v, preferred_element_type=F32)
        o_ref[rows, h * dv:(h + 1) * dv] = (o / l).astype(o_ref.dtype)


def _attention(q, k, v, *, n_q_heads, group, q_heads_per_step, dk, dv, k_head_offset, v_head_offset, bq,
               name, side_weights=()):
    b, s, _ = q.shape
    nq = q_heads_per_step
    nk = max(1, nq // group)
    assert (nq % group == 0 or group % nq == 0) and k_head_offset % nk == 0 and v_head_offset % nk == 0
    n_h = n_q_heads // nq
    chains = nq * (s // bq)
    blocks = [((s, nq * dk), BF16), ((s, nk * dk), BF16), ((s, nk * dv), BF16), ((s, nq * dv), BF16)]
    side_specs, side_shapes = [], []
    for w in side_weights:
        chunk = w.shape[0] // (b * n_h)
        assert chunk * b * n_h == w.shape[0]
        side_specs.append(pl.BlockSpec((chunk, w.shape[1]), lambda bi, h: (bi * n_h + h, 0)))
        side_shapes.append(jax.ShapeDtypeStruct(w.shape, BF16))
        blocks += [((chunk, w.shape[1]), F32), ((chunk, w.shape[1]), BF16)]
    out = pl.pallas_call(
        functools.partial(_attention_kernel, bq=bq, q_heads=nq, group=group, dk=dk, dv=dv,
                          n_side=len(side_weights)),
        out_shape=[jax.ShapeDtypeStruct((b, s, n_q_heads * dv), BF16)] + side_shapes,
        grid=(b, n_h),
        in_specs=[pl.BlockSpec((None, s, nq * dk), lambda bi, h: (bi, 0, h)),
                  pl.BlockSpec((None, s, nk * dk), lambda bi, h: (bi, 0, (k_head_offset + h * nq // group) // nk)),
                  pl.BlockSpec((None, s, nk * dv), lambda bi, h: (bi, 0, (v_head_offset + h * nq // group) // nk))]
                 + side_specs,
        out_specs=[pl.BlockSpec((None, s, nq * dv), lambda bi, h: (bi, 0, h))] + side_specs,
        compiler_params=_params(("parallel", "arbitrary"), blocks,
                                [((bq, s), F32)] * chains + [((bq, s), BF16)] * chains),
        name=name,
    )(q, k, v, *side_weights)
    return out[0], out[1:]


def _merge_kernel(oa_ref, ob_ref, wa_ref, wb_ref, ga_ref, gb_ref, o_ref):
    pa = _dot(oa_ref[...], wa_ref[...])
    pb = _dot(ob_ref[...], wb_ref[...])
    o_ref[...] = (ga_ref[...].astype(F32) * pa + gb_ref[...].astype(F32) * pb).astype(o_ref.dtype)


def _gated_merge(o_a, o_b, w_a, w_b, gates, gate_block_offset, bm, bn, name):
    t, ka = o_a.shape
    kb = o_b.shape[1]
    n = w_a.shape[1]
    nb = n // bn
    pipelined = [((ka, bn), w_a.dtype), ((kb, bn), w_b.dtype), ((bm, bn), BF16), ((bm, bn), BF16),
                 ((bm, bn), BF16)]
    single = [((bm, ka), BF16), ((bm, kb), BF16)] + [((bm, bn), F32)] * 2
    return pl.pallas_call(
        _merge_kernel,
        out_shape=jax.ShapeDtypeStruct((t, n), BF16),
        grid=(t // bm, nb),
        in_specs=[pl.BlockSpec((bm, ka), lambda i, j: (i, 0), pipeline_mode=_RESIDENT),
                  pl.BlockSpec((bm, kb), lambda i, j: (i, 0), pipeline_mode=_RESIDENT),
                  pl.BlockSpec((ka, bn), lambda i, j: (0, j)),
                  pl.BlockSpec((kb, bn), lambda i, j: (0, j)),
                  pl.BlockSpec((bm, bn), lambda i, j: (i, gate_block_offset + j)),
                  pl.BlockSpec((bm, bn), lambda i, j: (i, gate_block_offset + nb + j))],
        out_specs=pl.BlockSpec((bm, bn), lambda i, j: (i, j)),
        compiler_params=_params(("parallel", "arbitrary"), pipelined, single),
        name=name,
    )(o_a, o_b, w_a, w_b, gates, gates)


def _mm_residual_kernel(a_ref, b_ref, r_ref, o_ref):
    o_ref[...] = r_ref[...] + _dot(a_ref[...], b_ref[...])


def _matmul_residual(a, b, r, bm, bn, name, k_block_index, k_block):
    m = a.shape[0]
    k = k_block
    n = b.shape[1]
    pipelined = [((bm, k), a.dtype), ((k, bn), b.dtype), ((bm, bn), F32), ((bm, bn), F32)]
    single = [((bm, bn), F32)]
    return pl.pallas_call(
        _mm_residual_kernel,
        out_shape=jax.ShapeDtypeStruct((m, n), F32),
        grid=(m // bm, n // bn),
        in_specs=[pl.BlockSpec((bm, k), lambda i, j: (i, k_block_index)),
                  pl.BlockSpec((k, bn), lambda i, j: (k_block_index, j)),
                  pl.BlockSpec((bm, bn), lambda i, j: (i, j))],
        out_specs=pl.BlockSpec((bm, bn), lambda i, j: (i, j)),
        compiler_params=_params(("parallel", "arbitrary"), pipelined, single),
        name=name,
    )(a, b, r)


def _swiglu_kernel(a_ref, wg_ref, wu_ref, wd_ref, o_ref, wd16_ref):
    a = a_ref[...]
    g = _dot(a, wg_ref[...])
    u = _dot(a, wu_ref[...])
    o_ref[...] = (g * _sigmoid(g) * u).astype(o_ref.dtype)

    @pl.when(pl.program_id(0) == 0)
    def _():
        wd16_ref[...] = wd_ref[...].astype(wd16_ref.dtype)


def _swiglu(a, w_gate, w_up, w_down, bm, bn, name):
    t, k = a.shape
    n = w_gate.shape[1]
    nj = n // bn
    d_out = w_down.shape[1]
    chunk = w_down.shape[0] // nj

    def wd_index(i, j):
        return (jnp.where(i == 0, j, nj - 1), 0)

    pipelined = [((k, bn), w_gate.dtype), ((k, bn), w_up.dtype), ((bm, bn), BF16),
                 ((chunk, d_out), w_down.dtype), ((chunk, d_out), BF16)]
    single = [((bm, k), BF16)] + [((bm, bn), F32)] * 3
    return pl.pallas_call(
        _swiglu_kernel,
        out_shape=(jax.ShapeDtypeStruct((t, n), BF16), jax.ShapeDtypeStruct(w_down.shape, BF16)),
        grid=(t // bm, nj),
        in_specs=[pl.BlockSpec((bm, k), lambda i, j: (i, 0), pipeline_mode=_RESIDENT),
                  pl.BlockSpec((k, bn), lambda i, j: (0, j)),
                  pl.BlockSpec((k, bn), lambda i, j: (0, j)),
                  pl.BlockSpec((chunk, d_out), wd_index)],
        out_specs=(pl.BlockSpec((bm, bn), lambda i, j: (i, j)),
                   pl.BlockSpec((chunk, d_out), wd_index)),
        compiler_params=_params(("arbitrary", "arbitrary"), pipelined, single),
        name=name,
    )(a, w_gate, w_up, w_down)


def _layer(xf, batch, seq, g_attn, w_in, g_q_a, w_q_b, g_kv_a, w_kv_b, g_qn, g_kn,
           w_branch_a, w_branch_b, w_o, g_ffn, w_gate, w_up, w_down):
    t, d = xf.shape
    w_q = jnp.pad(w_q_b.reshape(Q_LORA, MLA_HEADS, MLA_NOPE + MLA_ROPE),
                  ((0, 0), (0, 0), (0, MLA_QK_PAD - MLA_NOPE - MLA_ROPE)))
    w_q = w_q.reshape(Q_LORA, MLA_HEADS * MLA_QK_PAD).astype(BF16)
    gqa_gains = jnp.concatenate([jnp.tile(g_qn, GQA_Q_HEADS), jnp.tile(g_kn, GQA_KV_HEADS)]).reshape(1, -1)
    gqa_post = jnp.concatenate([jnp.full((GQA_Q_HEADS * GQA_HEAD_DIM,), GQA_SCALE * LOG2_E, F32),
                                jnp.ones((GQA_KV_HEADS * GQA_HEAD_DIM,), F32)]).reshape(1, -1)
    mla_tables = _axial_tables(seq, MLA_ROPE)
    gqa_tables = _axial_tables(seq, GQA_HEAD_DIM)

    h = _rmsnorm(xf, g_attn, BF16, "norm_attn")
    w_in_t = w_in.T
    n_qkv = (GQA_Q_HEADS + 2 * GQA_KV_HEADS) * GQA_HEAD_DIM
    z, z_kpe = _in_proj_f32(h, w_in_t, LATENT_COLS + n_qkv, 2048, "in_proj_f32")
    gates = _in_proj_rows(_in_proj_sigmoid_kernel, h, w_in_t, LATENT_COLS + MLA_ROPE + n_qkv, 2 * d, BF16,
                          2048, "in_proj_gate")
    qkv_b = _gqa_prep(z, LATENT_COLS // IN_BLOCK, n_qkv, gqa_gains, gqa_post, gqa_tables, seq, 1024,
                      IN_BLOCK // GQA_HEAD_DIM, "gqa_prep")

    q_a = _mla_q_proj(z, g_q_a, w_q, mla_tables, seq, 1024, 4, "mla_q_proj")
    k_a, v_a = _mla_kv_proj(z, z_kpe, g_kv_a, w_kv_b, mla_tables, seq, 1024, 8, "mla_kv_proj")
    o_a, (w_o16,) = _attention(
        q_a.reshape(batch, seq, -1), k_a.reshape(batch, seq, -1), v_a.reshape(batch, seq, -1),
        n_q_heads=MLA_HEADS, group=1, q_heads_per_step=2, dk=MLA_QK_PAD, dv=MLA_V,
        k_head_offset=0, v_head_offset=0, bq=512, name="mla_attention", side_weights=(w_o,))

    qkv_b3 = qkv_b.reshape(batch, seq, -1)
    o_b, (w_a16, w_b16) = _attention(
        qkv_b3, qkv_b3, qkv_b3, n_q_heads=GQA_Q_HEADS,
        group=GQA_Q_HEADS // GQA_KV_HEADS, q_heads_per_step=2, dk=GQA_HEAD_DIM, dv=GQA_HEAD_DIM,
        k_head_offset=GQA_Q_HEADS, v_head_offset=GQA_Q_HEADS + GQA_KV_HEADS, bq=512, name="gqa_attention",
        side_weights=(w_branch_a, w_branch_b))

    m = _gated_merge(o_a.reshape(t, -1), o_b.reshape(t, -1), w_a16, w_b16, gates, 0, 2048, 512,
                     "gated_merge")
    x1 = _matmul_residual(m, w_o16, xf, 1024, 1024, "out_proj", k_block_index=0, k_block=d)

    h2 = _rmsnorm(x1, g_ffn, BF16, "norm_ffn")
    act, wd = _swiglu(h2, w_gate, w_up, w_down, 2048, 256, "ffn_gate_up")
    k_half = act.shape[1] // 2
    y = _matmul_residual(act, wd, x1, 1024, 512, "ffn_down_lo", k_block_index=0, k_block=k_half)
    return _matmul_residual(act, wd, y, 1024, 512, "ffn_down_hi", k_block_index=1, k_block=k_half)


def kernel(x, g_attn, w_in, g_q_a, w_q_b, g_kv_a, w_kv_b, g_qn, g_kn, w_branch_a, w_branch_b, w_o,
           g_ffn, w_gate, w_up, w_down, g_final):
    batch, seq, d = x.shape
    xf = x.reshape(batch * seq, d)
    for l in range(g_attn.shape[0]):
        xf = _layer(xf, batch, seq, g_attn[l], w_in[l], g_q_a[l], w_q_b[l], g_kv_a[l], w_kv_b[l],
                    g_qn[l], g_kn[l], w_branch_a[l], w_branch_b[l], w_o[l], g_ffn[l],
                    w_gate[l], w_up[l], w_down[l])
    return _rmsnorm(xf, g_final, F32, "norm_final").reshape(batch, seq, d)
```

```python
import functools

import jax
import jax.numpy as jnp
import numpy as np
from jax import lax
from jax.experimental import pallas as pl
from jax.experimental.pallas import tpu as pltpu

GRID_W = 64
ROPE_THETA = 10000.0
EPS = 1e-6

MLA_HEADS = 16
MLA_NOPE = 128
MLA_ROPE = 64
MLA_V = 128
MLA_QK_PAD = 256
Q_LORA = 1024
KV_LORA = 512
MLA_SCALE = (MLA_NOPE + MLA_ROPE) ** -0.5

GQA_Q_HEADS = 16
GQA_KV_HEADS = 4
GQA_HEAD_DIM = 128
GQA_SCALE = GQA_HEAD_DIM ** -0.5
LOG2_E = 1.4426950408889634

LANES = 128
SUBLANES = 8
V7X_VMEM_BYTES = 64 * 1024 * 1024
VMEM_CAP_BYTES = V7X_VMEM_BYTES - 6 * 1024 * 1024

IN_BLOCK = 512
TAIL_SPLIT = 2
ROW_CHAIN = 256
LATENT_COLS = Q_LORA + KV_LORA

BF16 = jnp.bfloat16
F32 = jnp.float32

_RESIDENT = pl.Buffered(1)
RESIDENT_MIN_BYTES = 16 * 1024 * 1024


def _nbytes(shape, dtype):
    return int(np.prod(shape)) * jnp.dtype(dtype).itemsize


def _params(semantics, pipelined_blocks, single_blocks=()):
    est = 2 * sum(_nbytes(s, d) for s, d in pipelined_blocks)
    est += sum(_nbytes(s, d) for s, d in single_blocks)
    limit = min(VMEM_CAP_BYTES, int(est * 1.25) + (4 << 20))
    return pltpu.CompilerParams(dimension_semantics=semantics, vmem_limit_bytes=limit)


def _dot(a, b):
    return jnp.dot(a, b.astype(BF16), preferred_element_type=F32)


def _sigmoid(x):
    return 0.5 * jnp.tanh(0.5 * x) + 0.5


def _rmsnorm_kernel(x_ref, g_ref, o_ref):
    x = x_ref[...]
    ms = jnp.mean(x * x, axis=-1, keepdims=True)
    o_ref[...] = (x * lax.rsqrt(ms + EPS) * g_ref[...]).astype(o_ref.dtype)


def _rmsnorm(x, g, out_dtype, name, block_rows=256):
    t, d = x.shape
    blocks = [((block_rows, d), F32), ((block_rows, d), out_dtype)]
    return pl.pallas_call(
        _rmsnorm_kernel,
        out_shape=jax.ShapeDtypeStruct((t, d), out_dtype),
        grid=(t // block_rows,),
        in_specs=[pl.BlockSpec((block_rows, d), lambda i: (i, 0)),
                  pl.BlockSpec((1, d), lambda i: (0, 0))],
        out_specs=pl.BlockSpec((block_rows, d), lambda i: (i, 0)),
        compiler_params=_params(("parallel",), blocks, [((block_rows, d), F32)] * 2),
        name=name,
    )(x, g.reshape(1, d))


def _rope_table(pos, dim):
    inv = ROPE_THETA ** (-jnp.arange(0, dim, 2, dtype=F32) / dim)
    ang = pos.astype(F32)[:, None] * inv[None, :]
    ang = jnp.concatenate([ang, ang], axis=-1)
    return jnp.cos(ang), jnp.sin(ang)


def _axial_tables(seq, rot_dim):
    rows = seq // GRID_W
    row_idx = jnp.repeat(jnp.arange(rows, dtype=jnp.int32), GRID_W)
    col_idx = jnp.tile(jnp.arange(GRID_W, dtype=jnp.int32), rows)
    half = rot_dim // 2
    cos_r, sin_r = _rope_table(row_idx, half)
    cos_c, sin_c = _rope_table(col_idx, half)
    cos = jnp.concatenate([cos_r, cos_c], axis=-1)
    sin = jnp.concatenate([sin_r, sin_c], axis=-1)
    first = (jnp.arange(rot_dim) % half) < (half // 2)
    sin_up = jnp.where(first[None, :], -sin, 0.0)
    sin_dn = jnp.where(first[None, :], 0.0, sin)
    pad = ((0, 0), (0, LANES - rot_dim))
    return tuple(jnp.pad(t, pad) for t in (cos, sin_up, sin_dn))


def _rope_lanes(x, cos, sin_up, sin_dn, quarter):
    up = pltpu.roll(x, LANES - quarter, 1)
    dn = pltpu.roll(x, quarter, 1)
    return x * cos + up * sin_up + dn * sin_dn


def _dot_nt(a, bt):
    return lax.dot_general(a, bt.astype(BF16), (((1,), (1,)), ((), ())), preferred_element_type=F32)


def _in_proj_f32_kernel(h_ref, wt_ref, wtk_ref, z_ref, zk_ref):
    z_ref[...] = _dot_nt(h_ref[...], wt_ref[...])

    @pl.when(pl.program_id(1) == 0)
    def _():
        lane = lax.broadcasted_iota(jnp.int32, (1, LANES), 1)
        zk_ref[...] = jnp.where(lane < MLA_ROPE, _dot_nt(h_ref[...], wtk_ref[...]), 0.0)


def _in_proj_f32(h, w_in_t, n_cols, bm, name):
    t, k = h.shape
    latent_blocks = LATENT_COLS // IN_BLOCK
    skip = MLA_ROPE // SUBLANES

    def wt_index(i, j):
        tiles = j * (IN_BLOCK // SUBLANES) + jnp.where(j >= latent_blocks, skip, 0)
        return (tiles * SUBLANES, 0)

    resident_h = _nbytes((bm, k), BF16) >= RESIDENT_MIN_BYTES
    pipelined = [((IN_BLOCK, k), F32), ((bm, IN_BLOCK), F32), ((bm, LANES), F32)]
    pipelined += [] if resident_h else [((bm, k), BF16)]
    single = [((LANES, k), F32), ((bm, IN_BLOCK), F32), ((IN_BLOCK, k), BF16)]
    single += [((bm, k), BF16)] if resident_h else []
    return pl.pallas_call(
        _in_proj_f32_kernel,
        out_shape=(jax.ShapeDtypeStruct((t, n_cols), F32), jax.ShapeDtypeStruct((t, LANES), F32)),
        grid=(t // bm, n_cols // IN_BLOCK),
        in_specs=[pl.BlockSpec((bm, k), lambda i, j: (i, 0), pipeline_mode=_RESIDENT if resident_h else None),
                  pl.BlockSpec((pl.Element(IN_BLOCK), pl.Element(k)), wt_index),
                  pl.BlockSpec((LANES, k), lambda i, j: (LATENT_COLS // LANES, 0), pipeline_mode=_RESIDENT)],
        out_specs=(pl.BlockSpec((bm, IN_BLOCK), lambda i, j: (i, j)),
                   pl.BlockSpec((bm, LANES), lambda i, j: (i, 0))),
        compiler_params=_params(("parallel", "arbitrary"), pipelined, single),
        name=name,
    )(h, w_in_t, w_in_t)


def _in_proj_sigmoid_kernel(h_ref, wt_ref, o_ref):
    o_ref[...] = _sigmoid(_dot_nt(h_ref[...], wt_ref[...])).astype(o_ref.dtype)


def _in_proj_rows(body, h, w_in_t, first_row, n_rows, out_dtype, bm, name):
    t, k = h.shape
    resident_h = _nbytes((bm, k), BF16) >= RESIDENT_MIN_BYTES
    pipelined = [((IN_BLOCK, k), F32), ((bm, IN_BLOCK), out_dtype)] + ([] if resident_h else [((bm, k), BF16)])
    single = [((bm, IN_BLOCK), F32), ((IN_BLOCK, k), BF16)] + ([((bm, k), BF16)] if resident_h else [])
    return pl.pallas_call(
        body,
        out_shape=jax.ShapeDtypeStruct((t, n_rows), out_dtype),
        grid=(t // bm, n_rows // IN_BLOCK),
        in_specs=[pl.BlockSpec((bm, k), lambda i, j: (i, 0), pipeline_mode=_RESIDENT if resident_h else None),
                  pl.BlockSpec((pl.Element(IN_BLOCK), pl.Element(k)),
                               lambda i, j: ((first_row // SUBLANES + j * (IN_BLOCK // SUBLANES)) * SUBLANES, 0))],
        out_specs=pl.BlockSpec((bm, IN_BLOCK), lambda i, j: (i, j)),
        compiler_params=_params(("parallel", "arbitrary"), pipelined, single),
        name=name,
    )(h, w_in_t)


def _lane_sum_mxu(x):
    ones = jnp.ones((LANES, LANES), BF16)
    hi = x.astype(BF16)
    lo = (x - hi.astype(F32)).astype(BF16)
    return jnp.dot(hi, ones, preferred_element_type=F32) + jnp.dot(lo, ones, preferred_element_type=F32)


def _gqa_prep_kernel(z_ref, g_ref, c_ref, cos_ref, su_ref, sd_ref, o_ref, *, heads_per_step, rope_blocks):
    @pl.when(pl.program_id(1) < rope_blocks)
    def _():
        cos, su, sd = cos_ref[...], su_ref[...], sd_ref[...]
        for h in range(heads_per_step):
            sl = slice(h * GQA_HEAD_DIM, (h + 1) * GQA_HEAD_DIM)
            x = z_ref[:, sl]
            ms = _lane_sum_mxu(x * x) * (1.0 / GQA_HEAD_DIM)
            y = x * lax.rsqrt(ms + EPS) * g_ref[:, sl]
            o_ref[:, sl] = (_rope_lanes(y, cos, su, sd, GQA_HEAD_DIM // 4) * c_ref[:, sl]).astype(o_ref.dtype)

    @pl.when(pl.program_id(1) >= rope_blocks)
    def _():
        o_ref[...] = z_ref[...].astype(o_ref.dtype)


def _gqa_prep(z, first_block, n_cols, gains, post_scale, tables, seq, br, heads_per_step, name):
    t = z.shape[0]
    bn = heads_per_step * GQA_HEAD_DIM
    rope_blocks = gains.shape[1] // bn
    s_blocks = seq // br
    tab_spec = pl.BlockSpec((br, LANES), lambda i, j: (i % s_blocks, 0))
    row_spec = pl.BlockSpec((1, bn), lambda i, j: (0, jnp.minimum(j, rope_blocks - 1)))
    blocks = [((br, bn), F32), ((br, bn), BF16)] + [((br, LANES), F32)] * 3
    return pl.pallas_call(
        functools.partial(_gqa_prep_kernel, heads_per_step=heads_per_step, rope_blocks=rope_blocks),
        out_shape=jax.ShapeDtypeStruct((t, n_cols), BF16),
        grid=(t // br, n_cols // bn),
        in_specs=[pl.BlockSpec((br, bn), lambda i, j: (i, first_block + j)), row_spec, row_spec,
                  tab_spec, tab_spec, tab_spec],
        out_specs=pl.BlockSpec((br, bn), lambda i, j: (i, j)),
        compiler_params=_params(("parallel", "arbitrary"), blocks, [((br, bn), F32)] * 2),
        name=name,
    )(z, gains, post_scale, *tables)


def _mla_q_kernel(z_ref, g_ref, w_ref, cos_ref, su_ref, sd_ref, o_ref, cq_ref, *, heads_per_step):
    @pl.when(pl.program_id(1) == 0)
    def _():
        z = z_ref[...]
        ms = jnp.mean(z * z, axis=-1, keepdims=True)
        cq_ref[...] = (z * lax.rsqrt(ms + EPS) * g_ref[...]).astype(cq_ref.dtype)

    w = w_ref[...]
    n_chains = o_ref.shape[0] // ROW_CHAIN
    for r in range(n_chains):
        rows = slice(r * ROW_CHAIN, (r + 1) * ROW_CHAIN)
        q = _dot(cq_ref[rows, :], w) * (MLA_SCALE * LOG2_E)
        cos, su, sd = cos_ref[rows, :], su_ref[rows, :], sd_ref[rows, :]
        for h in range(heads_per_step):
            base = h * MLA_QK_PAD
            o_ref[rows, base:base + MLA_NOPE] = q[:, base:base + MLA_NOPE].astype(o_ref.dtype)
            pe = q[:, base + MLA_NOPE:base + MLA_QK_PAD]
            o_ref[rows, base + MLA_NOPE:base + MLA_QK_PAD] = _rope_lanes(
                pe, cos, su, sd, MLA_ROPE // 4).astype(o_ref.dtype)


def _mla_q_proj(z_lat, g_q_a, w_q, tables, seq, bm, heads_per_step, name):
    t = z_lat.shape[0]
    bn = heads_per_step * MLA_QK_PAD
    n = w_q.shape[1]
    s_blocks = seq // bm
    tab_spec = pl.BlockSpec((bm, LANES), lambda i, j: (i % s_blocks, 0))
    blocks = [((bm, Q_LORA), F32), ((Q_LORA, bn), BF16), ((bm, bn), BF16)] + [((bm, LANES), F32)] * 3
    return pl.pallas_call(
        functools.partial(_mla_q_kernel, heads_per_step=heads_per_step),
        out_shape=jax.ShapeDtypeStruct((t, n), BF16),
        grid=(t // bm, n // bn),
        in_specs=[pl.BlockSpec((bm, Q_LORA), lambda i, j: (i, 0)),
                  pl.BlockSpec((1, Q_LORA), lambda i, j: (0, 0)),
                  pl.BlockSpec((Q_LORA, bn), lambda i, j: (0, j)),
                  tab_spec, tab_spec, tab_spec],
        out_specs=pl.BlockSpec((bm, bn), lambda i, j: (i, j)),
        scratch_shapes=[pltpu.VMEM((bm, Q_LORA), BF16)],
        compiler_params=_params(("parallel", "arbitrary"), blocks,
                                [((bm, Q_LORA), BF16), ((bm, bn), F32), ((bm, Q_LORA), F32)]),
        name=name,
    )(z_lat, g_q_a.reshape(1, Q_LORA), w_q, *tables)


def _mla_kv_kernel(z_ref, pe_ref, g_ref, w_ref, cos_ref, su_ref, sd_ref, k_ref, v_ref, ckv_ref, kpe_ref,
                   *, heads_per_step):
    @pl.when(pl.program_id(1) == 0)
    def _():
        z = z_ref[...]
        ms = jnp.mean(z * z, axis=-1, keepdims=True)
        ckv_ref[...] = (z * lax.rsqrt(ms + EPS) * g_ref[...]).astype(ckv_ref.dtype)
        kpe_ref[...] = _rope_lanes(pe_ref[...], cos_ref[...], su_ref[...], sd_ref[...],
                                   MLA_ROPE // 4).astype(kpe_ref.dtype)

    kv = _dot(ckv_ref[...], w_ref[...])
    for h in range(heads_per_step):
        src = h * (MLA_NOPE + MLA_V)
        dst = h * MLA_QK_PAD
        k_ref[:, dst:dst + MLA_NOPE] = kv[:, src:src + MLA_NOPE].astype(k_ref.dtype)
        k_ref[:, dst + MLA_NOPE:dst + MLA_QK_PAD] = kpe_ref[...]
        v_ref[:, h * MLA_V:(h + 1) * MLA_V] = kv[:, src + MLA_NOPE:src + MLA_NOPE + MLA_V].astype(v_ref.dtype)


def _mla_kv_proj(z_lat, z_kpe, g_kv_a, w_kv, tables, seq, bm, heads_per_step, name):
    t = z_lat.shape[0]
    bn = heads_per_step * (MLA_NOPE + MLA_V)
    steps = w_kv.shape[1] // bn
    s_blocks = seq // bm
    tab_spec = pl.BlockSpec((bm, LANES), lambda i, j: (i % s_blocks, 0))
    kva_block = Q_LORA // KV_LORA
    bk, bv = heads_per_step * MLA_QK_PAD, heads_per_step * MLA_V
    blocks = ([((bm, KV_LORA), F32), ((bm, LANES), F32), ((KV_LORA, bn), w_kv.dtype), ((bm, bk), BF16),
               ((bm, bv), BF16)] + [((bm, LANES), F32)] * 3)
    return pl.pallas_call(
        functools.partial(_mla_kv_kernel, heads_per_step=heads_per_step),
        out_shape=(jax.ShapeDtypeStruct((t, steps * bk), BF16), jax.ShapeDtypeStruct((t, steps * bv), BF16)),
        grid=(t // bm, steps),
        in_specs=[pl.BlockSpec((bm, KV_LORA), lambda i, j: (i, kva_block)),
                  pl.BlockSpec((bm, LANES), lambda i, j: (i, 0)),
                  pl.BlockSpec((1, KV_LORA), lambda i, j: (0, 0)),
                  pl.BlockSpec((KV_LORA, bn), lambda i, j: (0, j)),
                  tab_spec, tab_spec, tab_spec],
        out_specs=(pl.BlockSpec((bm, bk), lambda i, j: (i, j)),
                   pl.BlockSpec((bm, bv), lambda i, j: (i, j))),
        scratch_shapes=[pltpu.VMEM((bm, KV_LORA), BF16), pltpu.VMEM((bm, LANES), BF16)],
        compiler_params=_params(("parallel", "arbitrary"), blocks,
                                [((bm, KV_LORA), BF16), ((bm, bn), F32), ((bm, KV_LORA), F32)]),
        name=name,
    )(z_lat, z_kpe, g_kv_a.reshape(1, KV_LORA), w_kv, *tables)


def _attention_kernel(q_ref, k_ref, v_ref, *refs, bq, q_heads, group, dk, dv, n_side):
    side_in, o_ref, side_out = refs[:n_side], refs[n_side], refs[n_side + 1:]
    for w_ref, w16_ref in zip(side_in, side_out):
        w16_ref[...] = w_ref[...].astype(w16_ref.dtype)
    chains = [(h, r0, bq) for h in range(q_heads) for r0 in range(0, q_ref.shape[0], bq)]
    h_last, r_last, _ = chains.pop()
    part = bq // TAIL_SPLIT
    chains += [(h_last, r_last + c * part, part) for c in range(TAIL_SPLIT)]
    for h, r0, n in chains:
        g = h // group
        k = k_ref[:, g * dk:(g + 1) * dk]
        v = v_ref[:, g * dv:(g + 1) * dv]
        rows = slice(r0, r0 + n)
        s = lax.dot_general(q_ref[rows, h * dk:(h + 1) * dk], k, (((1,), (1,)), ((), ())),
                            preferred_element_type=F32)
        m = jnp.max(s, axis=-1, keepdims=True)
        p = jnp.exp2(s - m)
        l = jnp.sum(p, axis=-1, keepdims=True)
        o = jnp.dot(p.astype(BF16), v, preferred_element_type=F32)
        o_ref[rows, h * dv:(h + 1) * dv] = (o / l).astype(o_ref.dtype)


def _attention(q, k, v, *, n_q_heads, group, q_heads_per_step, dk, dv, k_head_offset, v_head_offset, bq,
               name, side_weights=()):
    b, s, _ = q.shape
    nq = q_heads_per_step
    nk = max(1, nq // group)
    assert (nq % group == 0 or group % nq == 0) and k_head_offset % nk == 0 and v_head_offset % nk == 0
    n_h = n_q_heads // nq
    chains = nq * (s // bq)
    blocks = [((s, nq * dk), BF16), ((s, nk * dk), BF16), ((s, nk * dv), BF16), ((s, nq * dv), BF16)]
    side_specs, side_shapes = [], []
    for w in side_weights:
        chunk = w.shape[0] // (b * n_h)
        assert chunk * b * n_h == w.shape[0]
        side_specs.append(pl.BlockSpec((chunk, w.shape[1]), lambda bi, h: (bi * n_h + h, 0)))
        side_shapes.append(jax.ShapeDtypeStruct(w.shape, BF16))
        blocks += [((chunk, w.shape[1]), F32), ((chunk, w.shape[1]), BF16)]
    out = pl.pallas_call(
        functools.partial(_attention_kernel, bq=bq, q_heads=nq, group=group, dk=dk, dv=dv,
                          n_side=len(side_weights)),
        out_shape=[jax.ShapeDtypeStruct((b, s, n_q_heads * dv), BF16)] + side_shapes,
        grid=(b, n_h),
        in_specs=[pl.BlockSpec((None, s, nq * dk), lambda bi, h: (bi, 0, h)),
                  pl.BlockSpec((None, s, nk * dk), lambda bi, h: (bi, 0, (k_head_offset + h * nq // group) // nk)),
                  pl.BlockSpec((None, s, nk * dv), lambda bi, h: (bi, 0, (v_head_offset + h * nq // group) // nk))]
                 + side_specs,
        out_specs=[pl.BlockSpec((None, s, nq * dv), lambda bi, h: (bi, 0, h))] + side_specs,
        compiler_params=_params(("parallel", "arbitrary"), blocks,
                                [((bq, s), F32)] * chains + [((bq, s), BF16)] * chains),
        name=name,
    )(q, k, v, *side_weights)
    return out[0], out[1:]


def _merge_kernel(oa_ref, ob_ref, wa_ref, wb_ref, ga_ref, gb_ref, o_ref):
    pa = _dot(oa_ref[...], wa_ref[...])
    pb = _dot(ob_ref[...], wb_ref[...])
    o_ref[...] = (ga_ref[...].astype(F32) * pa + gb_ref[...].astype(F32) * pb).astype(o_ref.dtype)


def _gated_merge(o_a, o_b, w_a, w_b, gates, gate_block_offset, bm, bn, name):
    t, ka = o_a.shape
    kb = o_b.shape[1]
    n = w_a.shape[1]
    nb = n // bn
    o_blocks = [((bm, ka), BF16), ((bm, kb), BF16)]
    resident_o = sum(_nbytes(s, d) for s, d in o_blocks) >= RESIDENT_MIN_BYTES
    o_mode = _RESIDENT if resident_o else None
    pipelined = [((ka, bn), w_a.dtype), ((kb, bn), w_b.dtype), ((bm, bn), BF16), ((bm, bn), BF16),
                 ((bm, bn), BF16)] + ([] if resident_o else o_blocks)
    single = [((bm, bn), F32)] * 2 + (o_blocks if resident_o else [])
    return pl.pallas_call(
        _merge_kernel,
        out_shape=jax.ShapeDtypeStruct((t, n), BF16),
        grid=(t // bm, nb),
        in_specs=[pl.BlockSpec((bm, ka), lambda i, j: (i, 0), pipeline_mode=o_mode),
                  pl.BlockSpec((bm, kb), lambda i, j: (i, 0), pipeline_mode=o_mode),
                  pl.BlockSpec((ka, bn), lambda i, j: (0, j)),
                  pl.BlockSpec((kb, bn), lambda i, j: (0, j)),
                  pl.BlockSpec((bm, bn), lambda i, j: (i, gate_block_offset + j)),
                  pl.BlockSpec((bm, bn), lambda i, j: (i, gate_block_offset + nb + j))],
        out_specs=pl.BlockSpec((bm, bn), lambda i, j: (i, j)),
        compiler_params=_params(("parallel", "arbitrary"), pipelined, single),
        name=name,
    )(o_a, o_b, w_a, w_b, gates, gates)


def _mm_residual_kernel(a_ref, b_ref, r_ref, o_ref):
    o_ref[...] = r_ref[...] + _dot(a_ref[...], b_ref[...])


def _matmul_residual(a, b, r, bm, bn, name, k_block_index, k_block):
    m = a.shape[0]
    k = k_block
    n = b.shape[1]
    pipelined = [((bm, k), a.dtype), ((k, bn), b.dtype), ((bm, bn), F32), ((bm, bn), F32)]
    single = [((bm, bn), F32)]
    return pl.pallas_call(
        _mm_residual_kernel,
        out_shape=jax.ShapeDtypeStruct((m, n), F32),
        grid=(m // bm, n // bn),
        in_specs=[pl.BlockSpec((bm, k), lambda i, j: (i, k_block_index)),
                  pl.BlockSpec((k, bn), lambda i, j: (k_block_index, j)),
                  pl.BlockSpec((bm, bn), lambda i, j: (i, j))],
        out_specs=pl.BlockSpec((bm, bn), lambda i, j: (i, j)),
        compiler_params=_params(("parallel", "arbitrary"), pipelined, single),
        name=name,
    )(a, b, r)


def _swiglu_kernel(a_ref, wg_ref, wu_ref, wd_ref, o_ref, wd16_ref):
    a = a_ref[...]
    g = _dot(a, wg_ref[...])
    u = _dot(a, wu_ref[...])
    o_ref[...] = (g * _sigmoid(g) * u).astype(o_ref.dtype)

    @pl.when(pl.program_id(0) == 0)
    def _():
        wd16_ref[...] = wd_ref[...].astype(wd16_ref.dtype)


def _swiglu(a, w_gate, w_up, w_down, bm, bn, name):
    t, k = a.shape
    n = w_gate.shape[1]
    nj = n // bn
    d_out = w_down.shape[1]
    chunk = w_down.shape[0] // nj

    def wd_index(i, j):
        return (jnp.where(i == 0, j, nj - 1), 0)

    pipelined = [((k, bn), w_gate.dtype), ((k, bn), w_up.dtype), ((bm, bn), BF16),
                 ((chunk, d_out), w_down.dtype), ((chunk, d_out), BF16)]
    single = [((bm, k), BF16)] + [((bm, bn), F32)] * 3
    return pl.pallas_call(
        _swiglu_kernel,
        out_shape=(jax.ShapeDtypeStruct((t, n), BF16), jax.ShapeDtypeStruct(w_down.shape, BF16)),
        grid=(t // bm, nj),
        in_specs=[pl.BlockSpec((bm, k), lambda i, j: (i, 0), pipeline_mode=_RESIDENT),
                  pl.BlockSpec((k, bn), lambda i, j: (0, j)),
                  pl.BlockSpec((k, bn), lambda i, j: (0, j)),
                  pl.BlockSpec((chunk, d_out), wd_index)],
        out_specs=(pl.BlockSpec((bm, bn), lambda i, j: (i, j)),
                   pl.BlockSpec((chunk, d_out), wd_index)),
        compiler_params=_params(("arbitrary", "arbitrary"), pipelined, single),
        name=name,
    )(a, w_gate, w_up, w_down)


def _layer(xf, batch, seq, g_attn, w_in, g_q_a, w_q_b, g_kv_a, w_kv_b, g_qn, g_kn,
           w_branch_a, w_branch_b, w_o, g_ffn, w_gate, w_up, w_down):
    t, d = xf.shape
    w_q = jnp.pad(w_q_b.reshape(Q_LORA, MLA_HEADS, MLA_NOPE + MLA_ROPE),
                  ((0, 0), (0, 0), (0, MLA_QK_PAD - MLA_NOPE - MLA_ROPE)))
    w_q = w_q.reshape(Q_LORA, MLA_HEADS * MLA_QK_PAD).astype(BF16)
    gqa_gains = jnp.concatenate([jnp.tile(g_qn, GQA_Q_HEADS), jnp.tile(g_kn, GQA_KV_HEADS)]).reshape(1, -1)
    gqa_post = jnp.concatenate([jnp.full((GQA_Q_HEADS * GQA_HEAD_DIM,), GQA_SCALE * LOG2_E, F32),
                                jnp.ones((GQA_KV_HEADS * GQA_HEAD_DIM,), F32)]).reshape(1, -1)
    mla_tables = _axial_tables(seq, MLA_ROPE)
    gqa_tables = _axial_tables(seq, GQA_HEAD_DIM)

    h = _rmsnorm(xf, g_attn, BF16, "norm_attn")
    w_in_t = w_in.T
    n_qkv = (GQA_Q_HEADS + 2 * GQA_KV_HEADS) * GQA_HEAD_DIM
    z, z_kpe = _in_proj_f32(h, w_in_t, LATENT_COLS + n_qkv, 1024, "in_proj_f32")
    gates = _in_proj_rows(_in_proj_sigmoid_kernel, h, w_in_t, LATENT_COLS + MLA_ROPE + n_qkv, 2 * d, BF16,
                          2048, "in_proj_gate")
    qkv_b = _gqa_prep(z, LATENT_COLS // IN_BLOCK, n_qkv, gqa_gains, gqa_post, gqa_tables, seq, 1024,
                      IN_BLOCK // GQA_HEAD_DIM, "gqa_prep")

    q_a = _mla_q_proj(z, g_q_a, w_q, mla_tables, seq, 1024, 4, "mla_q_proj")
    k_a, v_a = _mla_kv_proj(z, z_kpe, g_kv_a, w_kv_b, mla_tables, seq, 1024, 8, "mla_kv_proj")
    o_a, (w_o16,) = _attention(
        q_a.reshape(batch, seq, -1), k_a.reshape(batch, seq, -1), v_a.reshape(batch, seq, -1),
        n_q_heads=MLA_HEADS, group=1, q_heads_per_step=2, dk=MLA_QK_PAD, dv=MLA_V,
        k_head_offset=0, v_head_offset=0, bq=512, name="mla_attention", side_weights=(w_o,))

    qkv_b3 = qkv_b.reshape(batch, seq, -1)
    o_b, (w_a16, w_b16) = _attention(
        qkv_b3, qkv_b3, qkv_b3, n_q_heads=GQA_Q_HEADS,
        group=GQA_Q_HEADS // GQA_KV_HEADS, q_heads_per_step=2, dk=GQA_HEAD_DIM, dv=GQA_HEAD_DIM,
        k_head_offset=GQA_Q_HEADS, v_head_offset=GQA_Q_HEADS + GQA_KV_HEADS, bq=512, name="gqa_attention",
        side_weights=(w_branch_a, w_branch_b))

    m = _gated_merge(o_a.reshape(t, -1), o_b.reshape(t, -1), w_a16, w_b16, gates, 0, 1024, 512,
                     "gated_merge")
    x1 = _matmul_residual(m, w_o16, xf, 1024, 1024, "out_proj", k_block_index=0, k_block=d)

    h2 = _rmsnorm(x1, g_ffn, BF16, "norm_ffn")
    act, wd = _swiglu(h2, w_gate, w_up, w_down, 2048, 256, "ffn_gate_up")
    k_half = act.shape[1] // 2
    y = _matmul_residual(act, wd, x1, 1024, 512, "ffn_down_lo", k_block_index=0, k_block=k_half)
    return _matmul_residual(act, wd, y, 1024, 512, "ffn_down_hi", k_block_index=1, k_block=k_half)


def kernel(x, g_attn, w_in, g_q_a, w_q_b, g_kv_a, w_kv_b, g_qn, g_kn, w_branch_a, w_branch_b, w_o,
           g_ffn, w_gate, w_up, w_down, g_final):
    batch, seq, d = x.shape
    xf = x.reshape(batch * seq, d)
    for l in range(g_attn.shape[0]):
        xf = _layer(xf, batch, seq, g_attn[l], w_in[l], g_q_a[l], w_q_b[l], g_kv_a[l], w_kv_b[l],
                    g_qn[l], g_kn[l], w_branch_a[l], w_branch_b[l], w_o[l], g_ffn[l],
                    w_gate[l], w_up[l], w_down[l])
    return _rmsnorm(xf, g_final, F32, "norm_final").reshape(batch, seq, d)
```

```python
import functools

import jax
import jax.numpy as jnp
import numpy as np
from jax import lax
from jax.experimental import pallas as pl
from jax.experimental.pallas import tpu as pltpu

GRID_W = 64
ROPE_THETA = 10000.0
EPS = 1e-6

MLA_HEADS = 16
MLA_NOPE = 128
MLA_ROPE = 64
MLA_V = 128
MLA_QK_PAD = 256
Q_LORA = 1024
KV_LORA = 512
MLA_SCALE = (MLA_NOPE + MLA_ROPE) ** -0.5

GQA_Q_HEADS = 16
GQA_KV_HEADS = 4
GQA_HEAD_DIM = 128
GQA_SCALE = GQA_HEAD_DIM ** -0.5
LOG2_E = 1.4426950408889634

LANES = 128
SUBLANES = 8
V7X_VMEM_BYTES = 64 * 1024 * 1024
VMEM_CAP_BYTES = V7X_VMEM_BYTES - 6 * 1024 * 1024

IN_BLOCK = 512
TAIL_SPLIT = 2
ROW_CHAIN = 256
LATENT_COLS = Q_LORA + KV_LORA

BF16 = jnp.bfloat16
F32 = jnp.float32

_RESIDENT = pl.Buffered(1)
RESIDENT_MIN_BYTES = 16 * 1024 * 1024


def _nbytes(shape, dtype):
    return int(np.prod(shape)) * jnp.dtype(dtype).itemsize


def _params(semantics, pipelined_blocks, single_blocks=()):
    est = 2 * sum(_nbytes(s, d) for s, d in pipelined_blocks)
    est += sum(_nbytes(s, d) for s, d in single_blocks)
    limit = min(VMEM_CAP_BYTES, int(est * 1.25) + (4 << 20))
    return pltpu.CompilerParams(dimension_semantics=semantics, vmem_limit_bytes=limit)


def _dot(a, b):
    return jnp.dot(a, b.astype(BF16), preferred_element_type=F32)


def _sigmoid(x):
    return 0.5 * jnp.tanh(0.5 * x) + 0.5


def _rmsnorm_kernel(x_ref, g_ref, o_ref):
    x = x_ref[...]
    ms = jnp.mean(x * x, axis=-1, keepdims=True)
    o_ref[...] = (x * lax.rsqrt(ms + EPS) * g_ref[...]).astype(o_ref.dtype)


def _rmsnorm(x, g, out_dtype, name, block_rows=256):
    t, d = x.shape
    blocks = [((block_rows, d), F32), ((block_rows, d), out_dtype)]
    return pl.pallas_call(
        _rmsnorm_kernel,
        out_shape=jax.ShapeDtypeStruct((t, d), out_dtype),
        grid=(t // block_rows,),
        in_specs=[pl.BlockSpec((block_rows, d), lambda i: (i, 0)),
                  pl.BlockSpec((1, d), lambda i: (0, 0))],
        out_specs=pl.BlockSpec((block_rows, d), lambda i: (i, 0)),
        compiler_params=_params(("parallel",), blocks, [((block_rows, d), F32)] * 2),
        name=name,
    )(x, g.reshape(1, d))


def _rope_table(pos, dim):
    inv = ROPE_THETA ** (-jnp.arange(0, dim, 2, dtype=F32) / dim)
    ang = pos.astype(F32)[:, None] * inv[None, :]
    ang = jnp.concatenate([ang, ang], axis=-1)
    return jnp.cos(ang), jnp.sin(ang)


def _axial_tables(seq, rot_dim):
    rows = seq // GRID_W
    row_idx = jnp.repeat(jnp.arange(rows, dtype=jnp.int32), GRID_W)
    col_idx = jnp.tile(jnp.arange(GRID_W, dtype=jnp.int32), rows)
    half = rot_dim // 2
    cos_r, sin_r = _rope_table(row_idx, half)
    cos_c, sin_c = _rope_table(col_idx, half)
    cos = jnp.concatenate([cos_r, cos_c], axis=-1)
    sin = jnp.concatenate([sin_r, sin_c], axis=-1)
    first = (jnp.arange(rot_dim) % half) < (half // 2)
    sin_up = jnp.where(first[None, :], -sin, 0.0)
    sin_dn = jnp.where(first[None, :], 0.0, sin)
    pad = ((0, 0), (0, LANES - rot_dim))
    return tuple(jnp.pad(t, pad) for t in (cos, sin_up, sin_dn))


def _rope_lanes(x, cos, sin_up, sin_dn, quarter):
    up = pltpu.roll(x, LANES - quarter, 1)
    dn = pltpu.roll(x, quarter, 1)
    return x * cos + up * sin_up + dn * sin_dn


def _dot_nt(a, bt):
    return lax.dot_general(a, bt.astype(BF16), (((1,), (1,)), ((), ())), preferred_element_type=F32)


def _in_proj_f32_kernel(h_ref, wt_ref, wtk_ref, z_ref, zk_ref):
    z_ref[...] = _dot_nt(h_ref[...], wt_ref[...])

    @pl.when(pl.program_id(1) == 0)
    def _():
        lane = lax.broadcasted_iota(jnp.int32, (1, LANES), 1)
        zk_ref[...] = jnp.where(lane < MLA_ROPE, _dot_nt(h_ref[...], wtk_ref[...]), 0.0)


def _in_proj_f32(h, w_in_t, n_cols, bm, name):
    t, k = h.shape
    latent_blocks = LATENT_COLS // IN_BLOCK
    skip = MLA_ROPE // SUBLANES

    def wt_index(i, j):
        tiles = j * (IN_BLOCK // SUBLANES) + jnp.where(j >= latent_blocks, skip, 0)
        return (tiles * SUBLANES, 0)

    resident_h = _nbytes((bm, k), BF16) >= RESIDENT_MIN_BYTES
    pipelined = [((IN_BLOCK, k), F32), ((bm, IN_BLOCK), F32), ((bm, LANES), F32)]
    pipelined += [] if resident_h else [((bm, k), BF16)]
    single = [((LANES, k), F32), ((bm, IN_BLOCK), F32), ((IN_BLOCK, k), BF16)]
    single += [((bm, k), BF16)] if resident_h else []
    return pl.pallas_call(
        _in_proj_f32_kernel,
        out_shape=(jax.ShapeDtypeStruct((t, n_cols), F32), jax.ShapeDtypeStruct((t, LANES), F32)),
        grid=(t // bm, n_cols // IN_BLOCK),
        in_specs=[pl.BlockSpec((bm, k), lambda i, j: (i, 0), pipeline_mode=_RESIDENT if resident_h else None),
                  pl.BlockSpec((pl.Element(IN_BLOCK), pl.Element(k)), wt_index),
                  pl.BlockSpec((LANES, k), lambda i, j: (LATENT_COLS // LANES, 0), pipeline_mode=_RESIDENT)],
        out_specs=(pl.BlockSpec((bm, IN_BLOCK), lambda i, j: (i, j)),
                   pl.BlockSpec((bm, LANES), lambda i, j: (i, 0))),
        compiler_params=_params(("parallel", "arbitrary"), pipelined, single),
        name=name,
    )(h, w_in_t, w_in_t)


def _in_proj_sigmoid_kernel(h_ref, wt_ref, o_ref):
    half = o_ref.shape[0] // 2
    for r in range(2):
        rows = slice(r * half, (r + 1) * half)
        o_ref[rows, :] = _sigmoid(_dot_nt(h_ref[rows, :], wt_ref[...])).astype(o_ref.dtype)


def _in_proj_rows(body, h, w_in_t, first_row, n_rows, out_dtype, bm, name):
    t, k = h.shape
    resident_h = _nbytes((bm, k), BF16) >= RESIDENT_MIN_BYTES
    pipelined = [((IN_BLOCK, k), F32), ((bm, IN_BLOCK), out_dtype)] + ([] if resident_h else [((bm, k), BF16)])
    single = [((bm, IN_BLOCK), F32), ((IN_BLOCK, k), BF16)] + ([((bm, k), BF16)] if resident_h else [])
    return pl.pallas_call(
        body,
        out_shape=jax.ShapeDtypeStruct((t, n_rows), out_dtype),
        grid=(t // bm, n_rows // IN_BLOCK),
        in_specs=[pl.BlockSpec((bm, k), lambda i, j: (i, 0), pipeline_mode=_RESIDENT if resident_h else None),
                  pl.BlockSpec((pl.Element(IN_BLOCK), pl.Element(k)),
                               lambda i, j: ((first_row // SUBLANES + j * (IN_BLOCK // SUBLANES)) * SUBLANES, 0))],
        out_specs=pl.BlockSpec((bm, IN_BLOCK), lambda i, j: (i, j)),
        compiler_params=_params(("parallel", "arbitrary"), pipelined, single),
        name=name,
    )(h, w_in_t)


def _lane_sum_mxu(x):
    ones = jnp.ones((LANES, LANES), BF16)
    hi = x.astype(BF16)
    lo = (x - hi.astype(F32)).astype(BF16)
    return jnp.dot(hi, ones, preferred_element_type=F32) + jnp.dot(lo, ones, preferred_element_type=F32)


def _gqa_prep_kernel(z_ref, g_ref, c_ref, cos_ref, su_ref, sd_ref, o_ref, *, heads_per_step, rope_blocks):
    @pl.when(pl.program_id(1) < rope_blocks)
    def _():
        cos, su, sd = cos_ref[...], su_ref[...], sd_ref[...]
        for h in range(heads_per_step):
            sl = slice(h * GQA_HEAD_DIM, (h + 1) * GQA_HEAD_DIM)
            x = z_ref[:, sl]
            ms = _lane_sum_mxu(x * x) * (1.0 / GQA_HEAD_DIM)
            y = x * lax.rsqrt(ms + EPS) * g_ref[:, sl]
            o_ref[:, sl] = (_rope_lanes(y, cos, su, sd, GQA_HEAD_DIM // 4) * c_ref[:, sl]).astype(o_ref.dtype)

    @pl.when(pl.program_id(1) >= rope_blocks)
    def _():
        o_ref[...] = z_ref[...].astype(o_ref.dtype)


def _gqa_prep(z, first_block, n_cols, gains, post_scale, tables, seq, br, heads_per_step, name):
    t = z.shape[0]
    bn = heads_per_step * GQA_HEAD_DIM
    rope_blocks = gains.shape[1] // bn
    s_blocks = seq // br
    tab_spec = pl.BlockSpec((br, LANES), lambda i, j: (i % s_blocks, 0))
    row_spec = pl.BlockSpec((1, bn), lambda i, j: (0, jnp.minimum(j, rope_blocks - 1)))
    blocks = [((br, bn), F32), ((br, bn), BF16)] + [((br, LANES), F32)] * 3
    return pl.pallas_call(
        functools.partial(_gqa_prep_kernel, heads_per_step=heads_per_step, rope_blocks=rope_blocks),
        out_shape=jax.ShapeDtypeStruct((t, n_cols), BF16),
        grid=(t // br, n_cols // bn),
        in_specs=[pl.BlockSpec((br, bn), lambda i, j: (i, first_block + j)), row_spec, row_spec,
                  tab_spec, tab_spec, tab_spec],
        out_specs=pl.BlockSpec((br, bn), lambda i, j: (i, j)),
        compiler_params=_params(("parallel", "arbitrary"), blocks, [((br, bn), F32)] * 2),
        name=name,
    )(z, gains, post_scale, *tables)


def _mla_q_kernel(z_ref, g_ref, w_ref, cos_ref, su_ref, sd_ref, o_ref, cq_ref, *, heads_per_step):
    @pl.when(pl.program_id(1) == 0)
    def _():
        z = z_ref[...]
        ms = jnp.mean(z * z, axis=-1, keepdims=True)
        cq_ref[...] = (z * lax.rsqrt(ms + EPS) * g_ref[...]).astype(cq_ref.dtype)

    w = w_ref[...]
    n_chains = o_ref.shape[0] // ROW_CHAIN
    for r in range(n_chains):
        rows = slice(r * ROW_CHAIN, (r + 1) * ROW_CHAIN)
        q = _dot(cq_ref[rows, :], w) * (MLA_SCALE * LOG2_E)
        cos, su, sd = cos_ref[rows, :], su_ref[rows, :], sd_ref[rows, :]
        for h in range(heads_per_step):
            base = h * MLA_QK_PAD
            o_ref[rows, base:base + MLA_NOPE] = q[:, base:base + MLA_NOPE].astype(o_ref.dtype)
            pe = q[:, base + MLA_NOPE:base + MLA_QK_PAD]
            o_ref[rows, base + MLA_NOPE:base + MLA_QK_PAD] = _rope_lanes(
                pe, cos, su, sd, MLA_ROPE // 4).astype(o_ref.dtype)


def _mla_q_proj(z_lat, g_q_a, w_q, tables, seq, bm, heads_per_step, name):
    t = z_lat.shape[0]
    bn = heads_per_step * MLA_QK_PAD
    n = w_q.shape[1]
    s_blocks = seq // bm
    tab_spec = pl.BlockSpec((bm, LANES), lambda i, j: (i % s_blocks, 0))
    blocks = [((bm, Q_LORA), F32), ((Q_LORA, bn), BF16), ((bm, bn), BF16)] + [((bm, LANES), F32)] * 3
    return pl.pallas_call(
        functools.partial(_mla_q_kernel, heads_per_step=heads_per_step),
        out_shape=jax.ShapeDtypeStruct((t, n), BF16),
        grid=(t // bm, n // bn),
        in_specs=[pl.BlockSpec((bm, Q_LORA), lambda i, j: (i, 0)),
                  pl.BlockSpec((1, Q_LORA), lambda i, j: (0, 0)),
                  pl.BlockSpec((Q_LORA, bn), lambda i, j: (0, j)),
                  tab_spec, tab_spec, tab_spec],
        out_specs=pl.BlockSpec((bm, bn), lambda i, j: (i, j)),
        scratch_shapes=[pltpu.VMEM((bm, Q_LORA), BF16)],
        compiler_params=_params(("parallel", "arbitrary"), blocks,
                                [((bm, Q_LORA), BF16), ((bm, bn), F32), ((bm, Q_LORA), F32)]),
        name=name,
    )(z_lat, g_q_a.reshape(1, Q_LORA), w_q, *tables)


def _mla_kv_kernel(z_ref, pe_ref, g_ref, w_ref, cos_ref, su_ref, sd_ref, k_ref, v_ref, ckv_ref, kpe_ref,
                   *, heads_per_step):
    @pl.when(pl.program_id(1) == 0)
    def _():
        z = z_ref[...]
        ms = jnp.mean(z * z, axis=-1, keepdims=True)
        ckv_ref[...] = (z * lax.rsqrt(ms + EPS) * g_ref[...]).astype(ckv_ref.dtype)
        kpe_ref[...] = _rope_lanes(pe_ref[...], cos_ref[...], su_ref[...], sd_ref[...],
                                   MLA_ROPE // 4).astype(kpe_ref.dtype)

    kv = _dot(ckv_ref[...], w_ref[...])
    for h in range(heads_per_step):
        src = h * (MLA_NOPE + MLA_V)
        dst = h * MLA_QK_PAD
        k_ref[:, dst:dst + MLA_NOPE] = kv[:, src:src + MLA_NOPE].astype(k_ref.dtype)
        k_ref[:, dst + MLA_NOPE:dst + MLA_QK_PAD] = kpe_ref[...]
        v_ref[:, h * MLA_V:(h + 1) * MLA_V] = kv[:, src + MLA_NOPE:src + MLA_NOPE + MLA_V].astype(v_ref.dtype)


def _mla_kv_proj(z_lat, z_kpe, g_kv_a, w_kv, tables, seq, bm, heads_per_step, name):
    t = z_lat.shape[0]
    bn = heads_per_step * (MLA_NOPE + MLA_V)
    steps = w_kv.shape[1] // bn
    s_blocks = seq // bm
    tab_spec = pl.BlockSpec((bm, LANES), lambda i, j: (i % s_blocks, 0))
    kva_block = Q_LORA // KV_LORA
    bk, bv = heads_per_step * MLA_QK_PAD, heads_per_step * MLA_V
    blocks = ([((bm, KV_LORA), F32), ((bm, LANES), F32), ((KV_LORA, bn), w_kv.dtype), ((bm, bk), BF16),
               ((bm, bv), BF16)] + [((bm, LANES), F32)] * 3)
    return pl.pallas_call(
        functools.partial(_mla_kv_kernel, heads_per_step=heads_per_step),
        out_shape=(jax.ShapeDtypeStruct((t, steps * bk), BF16), jax.ShapeDtypeStruct((t, steps * bv), BF16)),
        grid=(t // bm, steps),
        in_specs=[pl.BlockSpec((bm, KV_LORA), lambda i, j: (i, kva_block)),
                  pl.BlockSpec((bm, LANES), lambda i, j: (i, 0)),
                  pl.BlockSpec((1, KV_LORA), lambda i, j: (0, 0)),
                  pl.BlockSpec((KV_LORA, bn), lambda i, j: (0, j)),
                  tab_spec, tab_spec, tab_spec],
        out_specs=(pl.BlockSpec((bm, bk), lambda i, j: (i, j)),
                   pl.BlockSpec((bm, bv), lambda i, j: (i, j))),
        scratch_shapes=[pltpu.VMEM((bm, KV_LORA), BF16), pltpu.VMEM((bm, LANES), BF16)],
        compiler_params=_params(("parallel", "arbitrary"), blocks,
                                [((bm, KV_LORA), BF16), ((bm, bn), F32), ((bm, KV_LORA), F32)]),
        name=name,
    )(z_lat, z_kpe, g_kv_a.reshape(1, KV_LORA), w_kv, *tables)


def _attention_kernel(q_ref, k_ref, v_ref, *refs, bq, q_heads, group, dk, dv, n_side):
    side_in, o_ref, side_out = refs[:n_side], refs[n_side], refs[n_side + 1:]
    for w_ref, w16_ref in zip(side_in, side_out):
        w16_ref[...] = w_ref[...].astype(w16_ref.dtype)
    chains = [(h, r0, bq) for h in range(q_heads) for r0 in range(0, q_ref.shape[0], bq)]
    h_last, r_last, _ = chains.pop()
    part = bq // TAIL_SPLIT
    chains += [(h_last, r_last + c * part, part) for c in range(TAIL_SPLIT)]
    for h, r0, n in chains:
        g = h // group
        k = k_ref[:, g * dk:(g + 1) * dk]
        v = v_ref[:, g * dv:(g + 1) * dv]
        rows = slice(r0, r0 + n)
        s = lax.dot_general(q_ref[rows, h * dk:(h + 1) * dk], k, (((1,), (1,)), ((), ())),
                            preferred_element_type=F32)
        m = jnp.max(s, axis=-1, keepdims=True)
        p = jnp.exp2(s - m)
        l = jnp.sum(p, axis=-1, keepdims=True)
        o = jnp.dot(p.astype(BF16), v, preferred_element_type=F32)
        o_ref[rows, h * dv:(h + 1) * dv] = (o / l).astype(o_ref.dtype)


def _attention(q, k, v, *, n_q_heads, group, q_heads_per_step, dk, dv, k_head_offset, v_head_offset, bq,
               name, side_weights=()):
    b, s, _ = q.shape
    nq = q_heads_per_step
    nk = max(1, nq // group)
    assert (nq % group == 0 or group % nq == 0) and k_head_offset % nk == 0 and v_head_offset % nk == 0
    n_h = n_q_heads // nq
    chains = nq * (s // bq)
    blocks = [((s, nq * dk), BF16), ((s, nk * dk), BF16), ((s, nk * dv), BF16), ((s, nq * dv), BF16)]
    side_specs, side_shapes = [], []
    for w in side_weights:
        chunk = w.shape[0] // (b * n_h)
        assert chunk * b * n_h == w.shape[0]
        side_specs.append(pl.BlockSpec((chunk, w.shape[1]), lambda bi, h: (bi * n_h + h, 0)))
        side_shapes.append(jax.ShapeDtypeStruct(w.shape, BF16))
        blocks += [((chunk, w.shape[1]), F32), ((chunk, w.shape[1]), BF16)]
    out = pl.pallas_call(
        functools.partial(_attention_kernel, bq=bq, q_heads=nq, group=group, dk=dk, dv=dv,
                          n_side=len(side_weights)),
        out_shape=[jax.ShapeDtypeStruct((b, s, n_q_heads * dv), BF16)] + side_shapes,
        grid=(b, n_h),
        in_specs=[pl.BlockSpec((None, s, nq * dk), lambda bi, h: (bi, 0, h)),
                  pl.BlockSpec((None, s, nk * dk), lambda bi, h: (bi, 0, (k_head_offset + h * nq // group) // nk)),
                  pl.BlockSpec((None, s, nk * dv), lambda bi, h: (bi, 0, (v_head_offset + h * nq // group) // nk))]
                 + side_specs,
        out_specs=[pl.BlockSpec((None, s, nq * dv), lambda bi, h: (bi, 0, h))] + side_specs,
        compiler_params=_params(("parallel", "arbitrary"), blocks,
                                [((bq, s), F32)] * chains + [((bq, s), BF16)] * chains),
        name=name,
    )(q, k, v, *side_weights)
    return out[0], out[1:]


def _merge_kernel(oa_ref, ob_ref, wa_ref, wb_ref, ga_ref, gb_ref, o_ref):
    pa = _dot(oa_ref[...], wa_ref[...])
    pb = _dot(ob_ref[...], wb_ref[...])
    o_ref[...] = (ga_ref[...].astype(F32) * pa + gb_ref[...].astype(F32) * pb).astype(o_ref.dtype)


def _gated_merge(o_a, o_b, w_a, w_b, gates, gate_block_offset, bm, bn, name):
    t, ka = o_a.shape
    kb = o_b.shape[1]
    n = w_a.shape[1]
    nb = n // bn
    o_blocks = [((bm, ka), BF16), ((bm, kb), BF16)]
    resident_o = sum(_nbytes(s, d) for s, d in o_blocks) >= RESIDENT_MIN_BYTES
    o_mode = _RESIDENT if resident_o else None
    pipelined = [((ka, bn), w_a.dtype), ((kb, bn), w_b.dtype), ((bm, bn), BF16), ((bm, bn), BF16),
                 ((bm, bn), BF16)] + ([] if resident_o else o_blocks)
    single = [((bm, bn), F32)] * 2 + (o_blocks if resident_o else [])
    return pl.pallas_call(
        _merge_kernel,
        out_shape=jax.ShapeDtypeStruct((t, n), BF16),
        grid=(t // bm, nb),
        in_specs=[pl.BlockSpec((bm, ka), lambda i, j: (i, 0), pipeline_mode=o_mode),
                  pl.BlockSpec((bm, kb), lambda i, j: (i, 0), pipeline_mode=o_mode),
                  pl.BlockSpec((ka, bn), lambda i, j: (0, j)),
                  pl.BlockSpec((kb, bn), lambda i, j: (0, j)),
                  pl.BlockSpec((bm, bn), lambda i, j: (i, gate_block_offset + j)),
                  pl.BlockSpec((bm, bn), lambda i, j: (i, gate_block_offset + nb + j))],
        out_specs=pl.BlockSpec((bm, bn), lambda i, j: (i, j)),
        compiler_params=_params(("parallel", "arbitrary"), pipelined, single),
        name=name,
    )(o_a, o_b, w_a, w_b, gates, gates)


def _mm_residual_kernel(a_ref, b_ref, r_ref, o_ref):
    o_ref[...] = r_ref[...] + _dot(a_ref[...], b_ref[...])


def _matmul_residual(a, b, r, bm, bn, name, k_block_index, k_block):
    m = a.shape[0]
    k = k_block
    n = b.shape[1]
    pipelined = [((bm, k), a.dtype), ((k, bn), b.dtype), ((bm, bn), F32), ((bm, bn), F32)]
    single = [((bm, bn), F32)]
    return pl.pallas_call(
        _mm_residual_kernel,
        out_shape=jax.ShapeDtypeStruct((m, n), F32),
        grid=(m // bm, n // bn),
        in_specs=[pl.BlockSpec((bm, k), lambda i, j: (i, k_block_index)),
                  pl.BlockSpec((k, bn), lambda i, j: (k_block_index, j)),
                  pl.BlockSpec((bm, bn), lambda i, j: (i, j))],
        out_specs=pl.BlockSpec((bm, bn), lambda i, j: (i, j)),
        compiler_params=_params(("parallel", "arbitrary"), pipelined, single),
        name=name,
    )(a, b, r)


def _swiglu_kernel(a_ref, wg_ref, wu_ref, wd_ref, o_ref, wd16_ref):
    half = o_ref.shape[0] // 2
    for r in range(2):
        rows = slice(r * half, (r + 1) * half)
        a = a_ref[rows, :]
        g = _dot(a, wg_ref[...])
        u = _dot(a, wu_ref[...])
        o_ref[rows, :] = (g * _sigmoid(g) * u).astype(o_ref.dtype)

    @pl.when(pl.program_id(0) == 0)
    def _():
        wd16_ref[...] = wd_ref[...].astype(wd16_ref.dtype)


def _swiglu(a, w_gate, w_up, w_down, bm, bn, name):
    t, k = a.shape
    n = w_gate.shape[1]
    nj = n // bn
    d_out = w_down.shape[1]
    chunk = w_down.shape[0] // nj

    def wd_index(i, j):
        return (jnp.where(i == 0, j, nj - 1), 0)

    pipelined = [((k, bn), w_gate.dtype), ((k, bn), w_up.dtype), ((bm, bn), BF16),
                 ((chunk, d_out), w_down.dtype), ((chunk, d_out), BF16)]
    single = [((bm, k), BF16)] + [((bm, bn), F32)] * 3
    return pl.pallas_call(
        _swiglu_kernel,
        out_shape=(jax.ShapeDtypeStruct((t, n), BF16), jax.ShapeDtypeStruct(w_down.shape, BF16)),
        grid=(t // bm, nj),
        in_specs=[pl.BlockSpec((bm, k), lambda i, j: (i, 0), pipeline_mode=_RESIDENT),
                  pl.BlockSpec((k, bn), lambda i, j: (0, j)),
                  pl.BlockSpec((k, bn), lambda i, j: (0, j)),
                  pl.BlockSpec((chunk, d_out), wd_index)],
        out_specs=(pl.BlockSpec((bm, bn), lambda i, j: (i, j)),
                   pl.BlockSpec((chunk, d_out), wd_index)),
        compiler_params=_params(("arbitrary", "arbitrary"), pipelined, single),
        name=name,
    )(a, w_gate, w_up, w_down)


def _layer(xf, batch, seq, g_attn, w_in, g_q_a, w_q_b, g_kv_a, w_kv_b, g_qn, g_kn,
           w_branch_a, w_branch_b, w_o, g_ffn, w_gate, w_up, w_down):
    t, d = xf.shape
    w_q = jnp.pad(w_q_b.reshape(Q_LORA, MLA_HEADS, MLA_NOPE + MLA_ROPE),
                  ((0, 0), (0, 0), (0, MLA_QK_PAD - MLA_NOPE - MLA_ROPE)))
    w_q = w_q.reshape(Q_LORA, MLA_HEADS * MLA_QK_PAD).astype(BF16)
    gqa_gains = jnp.concatenate([jnp.tile(g_qn, GQA_Q_HEADS), jnp.tile(g_kn, GQA_KV_HEADS)]).reshape(1, -1)
    gqa_post = jnp.concatenate([jnp.full((GQA_Q_HEADS * GQA_HEAD_DIM,), GQA_SCALE * LOG2_E, F32),
                                jnp.ones((GQA_KV_HEADS * GQA_HEAD_DIM,), F32)]).reshape(1, -1)
    mla_tables = _axial_tables(seq, MLA_ROPE)
    gqa_tables = _axial_tables(seq, GQA_HEAD_DIM)

    h = _rmsnorm(xf, g_attn, BF16, "norm_attn")
    w_in_t = w_in.T
    n_qkv = (GQA_Q_HEADS + 2 * GQA_KV_HEADS) * GQA_HEAD_DIM
    z, z_kpe = _in_proj_f32(h, w_in_t, LATENT_COLS + n_qkv, 1024, "in_proj_f32")
    gates = _in_proj_rows(_in_proj_sigmoid_kernel, h, w_in_t, LATENT_COLS + MLA_ROPE + n_qkv, 2 * d, BF16,
                          2048, "in_proj_gate")
    qkv_b = _gqa_prep(z, LATENT_COLS // IN_BLOCK, n_qkv, gqa_gains, gqa_post, gqa_tables, seq, 1024,
                      IN_BLOCK // GQA_HEAD_DIM, "gqa_prep")

    q_a = _mla_q_proj(z, g_q_a, w_q, mla_tables, seq, 1024, 4, "mla_q_proj")
    k_a, v_a = _mla_kv_proj(z, z_kpe, g_kv_a, w_kv_b, mla_tables, seq, 1024, 8, "mla_kv_proj")
    o_a, (w_o16,) = _attention(
        q_a.reshape(batch, seq, -1), k_a.reshape(batch, seq, -1), v_a.reshape(batch, seq, -1),
        n_q_heads=MLA_HEADS, group=1, q_heads_per_step=2, dk=MLA_QK_PAD, dv=MLA_V,
        k_head_offset=0, v_head_offset=0, bq=512, name="mla_attention", side_weights=(w_o,))

    qkv_b3 = qkv_b.reshape(batch, seq, -1)
    o_b, (w_a16, w_b16) = _attention(
        qkv_b3, qkv_b3, qkv_b3, n_q_heads=GQA_Q_HEADS,
        group=GQA_Q_HEADS // GQA_KV_HEADS, q_heads_per_step=2, dk=GQA_HEAD_DIM, dv=GQA_HEAD_DIM,
        k_head_offset=GQA_Q_HEADS, v_head_offset=GQA_Q_HEADS + GQA_KV_HEADS, bq=512, name="gqa_attention",
        side_weights=(w_branch_a, w_branch_b))

    m = _gated_merge(o_a.reshape(t, -1), o_b.reshape(t, -1), w_a16, w_b16, gates, 0, 1024, 512,
                     "gated_merge")
    x1 = _matmul_residual(m, w_o16, xf, 1024, 1024, "out_proj", k_block_index=0, k_block=d)

    h2 = _rmsnorm(x1, g_ffn, BF16, "norm_ffn")
    act, wd = _swiglu(h2, w_gate, w_up, w_down, 2048, 256, "ffn_gate_up")
    k_half = act.shape[1] // 2
    y = _matmul_residual(act, wd, x1, 1024, 512, "ffn_down_lo", k_block_index=0, k_block=k_half)
    return _matmul_residual(act, wd, y, 1024, 512, "ffn_down_hi", k_block_index=1, k_block=k_half)


def kernel(x, g_attn, w_in, g_q_a, w_q_b, g_kv_a, w_kv_b, g_qn, g_kn, w_branch_a, w_branch_b, w_o,
           g_ffn, w_gate, w_up, w_down, g_final):
    batch, seq, d = x.shape
    xf = x.reshape(batch * seq, d)
    for l in range(g_attn.shape[0]):
        xf = _layer(xf, batch, seq, g_attn[l], w_in[l], g_q_a[l], w_q_b[l], g_kv_a[l], w_kv_b[l],
                    g_qn[l], g_kn[l], w_branch_a[l], w_branch_b[l], w_o[l], g_ffn[l],
                    w_gate[l], w_up[l], w_down[l])
    return _rmsnorm(xf, g_final, F32, "norm_final").reshape(batch, seq, d)
```
